```python
import math
import jax
import jax.numpy as jnp
from jax import lax
import numpy as np

D_MODEL = 2048
BATCH = 2
SEQ = 4096
DEPTH = 4

N_MIXERS = 3
HEAD_DIM = 128
N_HEADS = D_MODEL // HEAD_DIM
N_KV_HEADS = 4
GQA_GROUP = N_HEADS // N_KV_HEADS
ATTN_WIDTH = N_HEADS * HEAD_DIM
KV_WIDTH = N_KV_HEADS * HEAD_DIM
WINDOW_A = 128
BLOCK_A = 128
DILATED_PATTERNS = ((128, 1), (512, 4), (2048, 16))
B_GROUP_HEADS = tuple(N_HEADS // 3 + (1 if g < N_HEADS % 3 else 0) for g in range(3))
BLOCK_B = 64
DIFF_HEAD_DIM = HEAD_DIM // 2
Q_BLOCK_C = 128
FFN_HIDDEN = -(-8 * D_MODEL // (3 * 256)) * 256
RMS_EPS = 1e-6
NEG_INF = -1e30

kernel_name = "hybrid_interleaved_encoder"


def _n_layers_with(m):
    return len(range(m, DEPTH, N_MIXERS))


def _rmsnorm(x, g):
    xf = x.astype(jnp.float32)
    y = xf * lax.rsqrt(jnp.mean(xf * xf, axis=-1, keepdims=True) + RMS_EPS)
    return (y * g.astype(jnp.float32)).astype(x.dtype)


def _alibi_slopes(n):
    return jnp.exp2(-8.0 * jnp.arange(1, n + 1, dtype=jnp.float32) / n)


def _banded_attention(q, k, v, slopes, block, half_window, dist_scale, sink=None):
    bt, L, kvh, grp, hd = q.shape
    nb = -(-L // block)
    lp = nb * block
    extra = lp - L
    q = jnp.pad(q, ((0, 0), (0, extra), (0, 0), (0, 0), (0, 0)))
    kp = jnp.pad(k, ((0, 0), (half_window, extra + half_window), (0, 0), (0, 0)))
    vp = jnp.pad(v, ((0, 0), (half_window, extra + half_window), (0, 0), (0, 0)))
    span = block + 2 * half_window
    idx = jnp.arange(nb)[:, None] * block + jnp.arange(span)[None, :]
    kb = kp[:, idx]
    vb = vp[:, idx]
    qb = q.reshape(bt, nb, block, kvh, grp, hd)
    logits = jnp.einsum('bnqhgd,bnkhd->bnhgqk', qb, kb).astype(jnp.float32) / math.sqrt(hd)
    rel = jnp.arange(block)[:, None] - jnp.arange(span)[None, :] + half_window
    kpos = idx - half_window
    valid = (jnp.abs(rel) <= half_window)[None] & ((kpos >= 0) & (kpos < L))[:, None, :]
    dist = (jnp.abs(rel) * dist_scale).astype(jnp.float32)
    logits = logits - slopes[:, :, None, None] * dist
    logits = jnp.where(valid[None, :, None, None], logits, NEG_INF)
    mx = jnp.max(logits, axis=-1)
    if sink is not None:
        mx = jnp.maximum(mx, sink[None, None, :, :, None])
    e = jnp.exp(logits - mx[..., None])
    denom = jnp.sum(e, axis=-1)
    if sink is not None:
        denom = denom + jnp.exp(sink[None, None, :, :, None] - mx)
    p = e / denom[..., None]
    out = jnp.einsum('bnhgqk,bnkhd->bnqhgd', p.astype(v.dtype), vb)
    out = out.reshape(bt, lp, kvh, grp, hd)[:, :L]
    lse = (mx + jnp.log(denom)).transpose(0, 1, 4, 2, 3).reshape(bt, lp, kvh, grp)[:, :L]
    return out, lse


def _mixer_window_gqa(h, w_qkv, w_o, sink):
    b, s, _ = h.shape
    q, k, v = jnp.split(h @ w_qkv, [ATTN_WIDTH, ATTN_WIDTH + KV_WIDTH], axis=-1)
    q = q.reshape(b, s, N_KV_HEADS, GQA_GROUP, HEAD_DIM)
    k = k.reshape(b, s, N_KV_HEADS, HEAD_DIM)
    v = v.reshape(b, s, N_KV_HEADS, HEAD_DIM)
    slopes = _alibi_slopes(N_HEADS).reshape(N_KV_HEADS, GQA_GROUP)
    sink_hg = sink.astype(jnp.float32).reshape(N_KV_HEADS, GQA_GROUP)
    out, _ = _banded_attention(q, k, v, slopes, BLOCK_A, WINDOW_A, 1, sink_hg)
    return out.reshape(b, s, ATTN_WIDTH) @ w_o


def _to_strided(t, dil):
    b, s, hg, hd = t.shape
    return t.reshape(b, s // dil, dil, hg, hd).transpose(0, 2, 1, 3, 4).reshape(b * dil, s // dil, hg, hd)


def _from_strided(t, b, dil):
    rest = t.shape[2:]
    t = t.reshape((b, dil) + t.shape[1:])
    perm = (0, 2, 1) + tuple(range(3, t.ndim))
    return t.transpose(perm).reshape((b, t.shape[2] * dil) + rest)


def _mixer_dilated(h, w_qkv, w_o):
    b, s, _ = h.shape
    q, k, v = (t.reshape(b, s, N_HEADS, HEAD_DIM) for t in jnp.split(h @ w_qkv, 3, axis=-1))
    slopes = _alibi_slopes(N_HEADS)
    outs, scores = [], []
    h0 = 0
    for (window, dil), hg in zip(DILATED_PATTERNS, B_GROUP_HEADS):
        half = window // (2 * dil)
        sl = slice(h0, h0 + hg)
        o, lse = _banded_attention(_to_strided(q[:, :, sl], dil)[:, :, :, None],
                                   _to_strided(k[:, :, sl], dil), _to_strided(v[:, :, sl], dil),
                                   slopes[sl][:, None], BLOCK_B, half, dil)
        o = _from_strided(o[:, :, :, 0], b, dil).reshape(b, s, hg * HEAD_DIM)
        lse = _from_strided(lse[..., 0], b, dil)
        outs.append(o)
        scores.append(jnp.mean(lse, axis=-1))
        h0 += hg
    alpha = jax.nn.softmax(jnp.stack(scores, axis=-1), axis=-1)
    mixed = jnp.concatenate([o * alpha[..., g:g + 1].astype(o.dtype) for g, o in enumerate(outs)], axis=-1)
    return mixed @ w_o


def _mixer_diff(h, w_qkv, w_o, lq1, lk1, lq2, lk2, subln_g, lambda_init):
    b, s, _ = h.shape
    q, k, v = jnp.split(h @ w_qkv, [ATTN_WIDTH, ATTN_WIDTH + KV_WIDTH], axis=-1)
    q = q.reshape(b, s, N_KV_HEADS, GQA_GROUP, 2, DIFF_HEAD_DIM)
    k = k.reshape(b, s, N_KV_HEADS, 2, DIFF_HEAD_DIM)
    v = v.reshape(b, s, N_KV_HEADS, 2 * DIFF_HEAD_DIM)
    f32 = jnp.float32
    lam = (jnp.exp(jnp.sum(lq1.astype(f32) * lk1.astype(f32)))
           - jnp.exp(jnp.sum(lq2.astype(f32) * lk2.astype(f32))) + lambda_init)
    slopes = _alibi_slopes(N_HEADS).reshape(N_KV_HEADS, GQA_GROUP)
    scale = DIFF_HEAD_DIM ** -0.5
    nb = s // Q_BLOCK_C
    kpos = jnp.arange(s)
    qblocks = q.reshape(b, nb, Q_BLOCK_C, N_KV_HEADS, GQA_GROUP, 2, DIFF_HEAD_DIM).transpose(1, 0, 2, 3, 4, 5, 6)

    def block(args):
        qb, j = args
        logits = jnp.einsum('bqhgcd,bkhcd->bhgcqk', qb, k).astype(f32) * scale
        qpos = j * Q_BLOCK_C + jnp.arange(Q_BLOCK_C)
        dist = jnp.abs(qpos[:, None] - kpos[None, :]).astype(f32)
        logits = logits - slopes[:, :, None, None, None] * dist
        p = jax.nn.softmax(logits, axis=-1)
        attn = p[:, :, :, 0] - lam * p[:, :, :, 1]
        return jnp.einsum('bhgqk,bkhd->bqhgd', attn.astype(v.dtype), v)

    o = lax.map(block, (qblocks, jnp.arange(nb)))
    o = o.transpose(1, 0, 2, 3, 4, 5).reshape(b, s, N_HEADS, 2 * DIFF_HEAD_DIM)
    o = _rmsnorm(o, subln_g) * (1.0 - lambda_init)
    return o.reshape(b, s, ATTN_WIDTH) @ w_o


def _swiglu(h, w_gate, w_up, w_down):
    return (jax.nn.silu(h @ w_gate) * (h @ w_up)) @ w_down


def setup_inputs(seed: int = 0) -> dict:
    key = jax.random.key(seed)
    ks = jax.random.split(key, 20)
    f32 = jnp.float32
    n_a, n_b, n_c = (_n_layers_with(m) for m in range(N_MIXERS))

    def w(k, shape, fan_in):
        return jax.random.normal(k, shape, f32) * fan_in ** -0.5

    def gain(k, shape):
        return 1.0 + 0.1 * jax.random.normal(k, shape, f32)

    return {
        "x": jax.random.normal(ks[0], (BATCH, SEQ, D_MODEL), f32),
        "mix_pre_g": gain(ks[1], (DEPTH, D_MODEL)),
        "mix_post_g": gain(ks[2], (DEPTH, D_MODEL)),
        "ffn_pre_g": gain(ks[3], (DEPTH, D_MODEL)),
        "ffn_post_g": gain(ks[4], (DEPTH, D_MODEL)),
        "w_gate": w(ks[5], (DEPTH, D_MODEL, FFN_HIDDEN), D_MODEL),
        "w_up": w(ks[6], (DEPTH, D_MODEL, FFN_HIDDEN), D_MODEL),
        "w_down": w(ks[7], (DEPTH, FFN_HIDDEN, D_MODEL), FFN_HIDDEN),
        "a_w_qkv": w(ks[8], (n_a, D_MODEL, ATTN_WIDTH + 2 * KV_WIDTH), D_MODEL),
        "a_w_o": w(ks[9], (n_a, ATTN_WIDTH, D_MODEL), ATTN_WIDTH),
        "a_sink": jax.random.normal(ks[10], (n_a, N_HEADS), f32),
        "b_w_qkv": w(ks[11], (n_b, D_MODEL, 3 * ATTN_WIDTH), D_MODEL),
        "b_w_o": w(ks[12], (n_b, ATTN_WIDTH, D_MODEL), ATTN_WIDTH),
        "c_w_qkv": w(ks[13], (n_c, D_MODEL, ATTN_WIDTH + 2 * KV_WIDTH), D_MODEL),
        "c_w_o": w(ks[14], (n_c, ATTN_WIDTH, D_MODEL), ATTN_WIDTH),
        "c_lambda_q1": 0.1 * jax.random.normal(ks[15], (n_c, DIFF_HEAD_DIM), f32),
        "c_lambda_k1": 0.1 * jax.random.normal(ks[16], (n_c, DIFF_HEAD_DIM), f32),
        "c_lambda_q2": 0.1 * jax.random.normal(ks[17], (n_c, DIFF_HEAD_DIM), f32),
        "c_lambda_k2": 0.1 * jax.random.normal(ks[18], (n_c, DIFF_HEAD_DIM), f32),
        "c_subln_g": gain(ks[19], (n_c, 2 * DIFF_HEAD_DIM)),
    }


def reference(x, mix_pre_g, mix_post_g, ffn_pre_g, ffn_post_g, w_gate, w_up, w_down,
              a_w_qkv, a_w_o, a_sink, b_w_qkv, b_w_o, c_w_qkv, c_w_o,
              c_lambda_q1, c_lambda_k1, c_lambda_q2, c_lambda_k2, c_subln_g):
    for i in range(DEPTH):
        m, j = i % N_MIXERS, i // N_MIXERS
        h = _rmsnorm(x, mix_pre_g[i])
        if m == 0:
            y = _mixer_window_gqa(h, a_w_qkv[j], a_w_o[j], a_sink[j])
        elif m == 1:
            y = _mixer_dilated(h, b_w_qkv[j], b_w_o[j])
        else:
            y = _mixer_diff(h, c_w_qkv[j], c_w_o[j], c_lambda_q1[j], c_lambda_k1[j],
                            c_lambda_q2[j], c_lambda_k2[j], c_subln_g[j],
                            0.8 - 0.6 * math.exp(-0.3 * i))
        x = x + _rmsnorm(y, mix_post_g[i])
        h = _rmsnorm(x, ffn_pre_g[i])
        x = x + _rmsnorm(_swiglu(h, w_gate[i], w_up[i], w_down[i]), ffn_post_g[i])
    return x
```

```python
import functools
import math

import jax
import jax.numpy as jnp
from jax import lax
from jax.experimental import pallas as pl
from jax.experimental.pallas import tpu as pltpu

F32 = jnp.float32
BF16 = jnp.bfloat16

D_MODEL = 2048
DEPTH = 4
N_MIXERS = 3
HEAD_DIM = 128
N_HEADS = 16
N_KV_HEADS = 4
GQA_GROUP = N_HEADS // N_KV_HEADS
DIFF_HEAD_DIM = HEAD_DIM // 2
WINDOW_A = 128
DILATED_PATTERNS = ((128, 1), (512, 4), (2048, 16))
B_GROUP_HEADS = (6, 5, 5)
FFN_HIDDEN = 5632
RMS_EPS = 1e-6
NEG_INF = -1e30

VMEM_LIMIT_BYTES = 48 * 1024 * 1024

PROJ_TM, PROJ_TN = 512, 1024
OUT_TM = 512
FFN_TM, FFN_TH = 512, 512
ATTN_BQ = 256
DIFF_BQ = 128


def _alibi_slopes():
    return jnp.exp2(-8.0 * jnp.arange(1, N_HEADS + 1, dtype=F32) / N_HEADS)


def _rms(x, g):
    return x * lax.rsqrt(jnp.mean(x * x, axis=-1, keepdims=True) + RMS_EPS) * g


def _params(*sem):
    return pltpu.CompilerParams(dimension_semantics=sem, vmem_limit_bytes=VMEM_LIMIT_BYTES)


def _norm_matmul_kernel(x_ref, g_ref, w_ref, o_ref, h_ref):
    @pl.when(pl.program_id(1) == 0)
    def _():
        h_ref[...] = _rms(x_ref[...], g_ref[...]).astype(BF16)

    o_ref[...] = jnp.dot(h_ref[...], w_ref[...], preferred_element_type=F32).astype(o_ref.dtype)


def _norm_matmul(x, g, w):
    n, dm = x.shape
    m = w.shape[1]
    return pl.pallas_call(
        _norm_matmul_kernel,
        out_shape=jax.ShapeDtypeStruct((n, m), BF16),
        grid=(n // PROJ_TM, m // PROJ_TN),
        in_specs=[
            pl.BlockSpec((PROJ_TM, dm), lambda i, j: (i, 0)),
            pl.BlockSpec((1, dm), lambda i, j: (0, 0)),
            pl.BlockSpec((dm, PROJ_TN), lambda i, j: (0, j)),
        ],
        out_specs=pl.BlockSpec((PROJ_TM, PROJ_TN), lambda i, j: (i, j)),
        scratch_shapes=[pltpu.VMEM((PROJ_TM, dm), BF16)],
        compiler_params=_params("parallel", "arbitrary"),
        name="norm_matmul",
    )(x, g, w)


def _out_kernel(o_ref, w_ref, g_ref, x_ref, out_ref):
    y = jnp.dot(o_ref[...], w_ref[...], preferred_element_type=F32)
    out_ref[...] = x_ref[...] + _rms(y, g_ref[...])


def _out_merge_kernel(o_ref, lse_ref, w_ref, g_ref, x_ref, out_ref):
    scores = []
    h0 = 0
    for hg in B_GROUP_HEADS:
        acc = lse_ref[:, h0 * HEAD_DIM:(h0 + 1) * HEAD_DIM]
        for h in range(h0 + 1, h0 + hg):
            acc = acc + lse_ref[:, h * HEAD_DIM:(h + 1) * HEAD_DIM]
        scores.append(acc * (1.0 / hg))
        h0 += hg
    mx = jnp.maximum(jnp.maximum(scores[0], scores[1]), scores[2])
    es = [jnp.exp(s - mx) for s in scores]
    inv = 1.0 / (es[0] + es[1] + es[2])
    pieces = []
    h0 = 0
    for gi, hg in enumerate(B_GROUP_HEADS):
        alpha = es[gi] * inv
        for h in range(h0, h0 + hg):
            oh = o_ref[:, h * HEAD_DIM:(h + 1) * HEAD_DIM].astype(F32)
            pieces.append((oh * alpha).astype(BF16))
        h0 += hg
    mixed = jnp.concatenate(pieces, axis=-1)
    y = jnp.dot(mixed, w_ref[...], preferred_element_type=F32)
    out_ref[...] = x_ref[...] + _rms(y, g_ref[...])


def _out_proj(o, w, g, x, lse=None):
    n, dm = x.shape
    row = lambda i: (i, 0)
    fixed = lambda i: (0, 0)
    o_spec = pl.BlockSpec((OUT_TM, dm), row)
    w_spec = pl.BlockSpec((dm, dm), fixed)
    g_spec = pl.BlockSpec((1, dm), fixed)
    if lse is None:
        body, args = _out_kernel, (o, w, g, x)
        in_specs = [o_spec, w_spec, g_spec, o_spec]
    else:
        body, args = _out_merge_kernel, (o, lse, w, g, x)
        in_specs = [o_spec, o_spec, w_spec, g_spec, o_spec]
    return pl.pallas_call(
        body,
        out_shape=jax.ShapeDtypeStruct((n, dm), F32),
        grid=(n // OUT_TM,),
        in_specs=in_specs,
        out_specs=o_spec,
        compiler_params=_params("parallel"),
        name="out_proj",
    )(*args)


def _ffn_kernel(x_ref, gpre_ref, wg_ref, wu_ref, wd_ref, gpost_ref, out_ref, h_ref, acc_ref):
    k = pl.program_id(1)

    @pl.when(k == 0)
    def _():
        h_ref[...] = _rms(x_ref[...], gpre_ref[...]).astype(BF16)

    h = h_ref[...]
    gate = jnp.dot(h, wg_ref[...], preferred_element_type=F32)
    up = jnp.dot(h, wu_ref[...], preferred_element_type=F32)
    act = (gate * jax.nn.sigmoid(gate) * up).astype(BF16)
    part = jnp.dot(act, wd_ref[...], preferred_element_type=F32)

    @pl.when(k == 0)
    def _():
        acc_ref[...] = part

    @pl.when(k > 0)
    def _():
        acc_ref[...] += part

    @pl.when(k == pl.num_programs(1) - 1)
    def _():
        out_ref[...] = x_ref[...] + _rms(acc_ref[...], gpost_ref[...])


def _ffn(x, gpre, wg, wu, wd, gpost):
    n, dm = x.shape
    hid = wg.shape[1]
    row = lambda i, k: (i, 0)
    fixed = lambda i, k: (0, 0)
    return pl.pallas_call(
        _ffn_kernel,
        out_shape=jax.ShapeDtypeStruct((n, dm), F32),
        grid=(n // FFN_TM, hid // FFN_TH),
        in_specs=[
            pl.BlockSpec((FFN_TM, dm), row),
            pl.BlockSpec((1, dm), fixed),
            pl.BlockSpec((dm, FFN_TH), lambda i, k: (0, k)),
            pl.BlockSpec((dm, FFN_TH), lambda i, k: (0, k)),
            pl.BlockSpec((FFN_TH, dm), lambda i, k: (k, 0)),
            pl.BlockSpec((1, dm), fixed),
        ],
        out_specs=pl.BlockSpec((FFN_TM, dm), row),
        scratch_shapes=[pltpu.VMEM((FFN_TM, dm), BF16), pltpu.VMEM((FFN_TM, dm), F32)],
        compiler_params=_params("parallel", "arbitrary"),
        name="ffn",
    )(x, gpre, wg, wu, wd, gpost)


def _band_attn_kernel(slope_ref, sink_ref, q_ref, k_ref, v_ref, *out_refs,
                      half_window, dilation, span, pad, head0, use_sink):
    o_ref = out_refs[0]
    bq = q_ref.shape[0]
    seq = k_ref.shape[0]
    head = pl.program_id(1) + head0
    t0 = pl.program_id(2) * bq
    start = pl.multiple_of(jnp.clip(t0 - pad, 0, seq - span), 128)

    q = q_ref[...]
    k = k_ref[pl.ds(start, span), :]
    v = v_ref[pl.ds(start, span), :]
    s = lax.dot_general(q, k, (((1,), (1,)), ((), ())), preferred_element_type=F32)
    s = s * (1.0 / math.sqrt(HEAD_DIM))
    row = lax.broadcasted_iota(jnp.int32, (bq, span), 0) + t0
    col = lax.broadcasted_iota(jnp.int32, (bq, span), 1) + start
    diff = row - col
    dist = jnp.abs(diff)
    valid = dist <= half_window
    if dilation > 1:
        valid = valid & ((diff & (dilation - 1)) == 0)
    s = s - slope_ref[head] * dist.astype(F32)
    s = jnp.where(valid, s, NEG_INF)
    mx = jnp.max(s, axis=-1, keepdims=True)
    if use_sink:
        mx = jnp.maximum(mx, sink_ref[head])
    e = jnp.exp(s - mx)
    den = jnp.sum(e, axis=-1, keepdims=True)
    if use_sink:
        den = den + jnp.exp(sink_ref[head] - mx)
    pv = jnp.dot(e.astype(BF16), v, preferred_element_type=F32)
    o_ref[...] = (pv / den).astype(o_ref.dtype)
    if len(out_refs) > 1:
        lse = mx + jnp.log(den)
        out_refs[1][...] = jnp.broadcast_to(lse, out_refs[1].shape)


def _band_attn(qkv, slopes, sink, *, q_col, k_col, v_col, head0, n_heads, kv_group,
               half_window, dilation, out_o, out_lse=None):
    bsz, seq, _ = qkv.shape
    bq = ATTN_BQ
    pad = -(-half_window // 128) * 128
    span = min(bq + 2 * pad, seq)
    use_sink = sink is not None
    if sink is None:
        sink = slopes
    want_lse = out_lse is not None

    q_spec = pl.BlockSpec((None, bq, HEAD_DIM), lambda b, h, n: (b, n, q_col + head0 + h))
    k_spec = pl.BlockSpec((None, seq, HEAD_DIM), lambda b, h, n: (b, 0, k_col + (head0 + h) // kv_group))
    v_spec = pl.BlockSpec((None, seq, HEAD_DIM), lambda b, h, n: (b, 0, v_col + (head0 + h) // kv_group))
    o_spec = pl.BlockSpec((None, bq, HEAD_DIM), lambda b, h, n: (b, n, head0 + h))
    smem = pl.BlockSpec(memory_space=pltpu.SMEM)
    any_spec = pl.BlockSpec(memory_space=pl.ANY)

    out_shape = [jax.ShapeDtypeStruct(out_o.shape, out_o.dtype)]
    out_specs = [o_spec]
    in_specs = [smem, smem, q_spec, k_spec, v_spec, any_spec]
    args = [slopes, sink, qkv, qkv, qkv, out_o]
    aliases = {5: 0}
    if want_lse:
        out_shape.append(jax.ShapeDtypeStruct(out_lse.shape, out_lse.dtype))
        out_specs.append(o_spec)
        in_specs.append(any_spec)
        args.append(out_lse)
        aliases[6] = 1

    def body(slope_ref, sink_ref, q_ref, k_ref, v_ref, *rest):
        outs = rest[len(rest) // 2:]
        _band_attn_kernel(slope_ref, sink_ref, q_ref, k_ref, v_ref, *outs,
                          half_window=half_window, dilation=dilation, span=span, pad=pad,
                          head0=head0, use_sink=use_sink)

    res = pl.pallas_call(
        body,
        out_shape=out_shape,
        grid=(bsz, n_heads, seq // bq),
        in_specs=in_specs,
        out_specs=out_specs,
        input_output_aliases=aliases,
        compiler_params=_params("parallel", "arbitrary", "arbitrary"),
        name=f"band_attn_w{half_window}_d{dilation}",
    )(*args)
    return res


def _diff_attn_kernel(slope_ref, lq1_ref, lk1_ref, lq2_ref, lk2_ref, sg_ref, q_ref, k_ref, v_ref,
                      o_ref, *, lambda_init):
    bq = q_ref.shape[0]
    seq = k_ref.shape[0]
    head = pl.program_id(1)
    t0 = pl.program_id(2) * bq

    lam = (jnp.exp(jnp.sum(lq1_ref[...] * lk1_ref[...], axis=-1, keepdims=True))
           - jnp.exp(jnp.sum(lq2_ref[...] * lk2_ref[...], axis=-1, keepdims=True))
           + lambda_init)

    q = q_ref[...]
    lane = lax.broadcasted_iota(jnp.int32, q.shape, 1)
    zero = jnp.zeros_like(q)
    q1 = jnp.where(lane < DIFF_HEAD_DIM, q, zero)
    q2 = jnp.where(lane >= DIFF_HEAD_DIM, q, zero)
    k = k_ref[...]
    row = lax.broadcasted_iota(jnp.int32, (bq, seq), 0) + t0
    col = lax.broadcasted_iota(jnp.int32, (bq, seq), 1)
    bias = slope_ref[head] * jnp.abs(row - col).astype(F32)
    scale = DIFF_HEAD_DIM ** -0.5
    dn = (((1,), (1,)), ((), ()))

    def softmax(qc):
        s = lax.dot_general(qc, k, dn, preferred_element_type=F32) * scale - bias
        mx = jnp.max(s, axis=-1, keepdims=True)
        e = jnp.exp(s - mx)
        return e / jnp.sum(e, axis=-1, keepdims=True)

    attn = softmax(q1) - lam * softmax(q2)
    o = jnp.dot(attn.astype(BF16), v_ref[...], preferred_element_type=F32)
    o = _rms(o, sg_ref[...]) * (1.0 - lambda_init)
    o_ref[...] = o.astype(o_ref.dtype)


def _diff_attn(qkv, slopes, lq1, lk1, lq2, lk2, subln_g, lambda_init):
    bsz, seq, _ = qkv.shape
    bq = DIFF_BQ
    q_spec = pl.BlockSpec((None, bq, HEAD_DIM), lambda b, h, n: (b, n, h))
    k_spec = pl.BlockSpec((None, seq, HEAD_DIM), lambda b, h, n: (b, 0, N_HEADS + h // GQA_GROUP))
    v_spec = pl.BlockSpec((None, seq, HEAD_DIM),
                          lambda b, h, n: (b, 0, N_HEADS + N_KV_HEADS + h // GQA_GROUP))
    vec = lambda w: pl.BlockSpec((1, w), lambda b, h, n: (0, 0))
    return pl.pallas_call(
        functools.partial(_diff_attn_kernel, lambda_init=lambda_init),
        out_shape=jax.ShapeDtypeStruct((bsz, seq, N_HEADS * HEAD_DIM), BF16),
        grid=(bsz, N_HEADS, seq // bq),
        in_specs=[pl.BlockSpec(memory_space=pltpu.SMEM),
                  vec(DIFF_HEAD_DIM), vec(DIFF_HEAD_DIM), vec(DIFF_HEAD_DIM), vec(DIFF_HEAD_DIM),
                  vec(HEAD_DIM), q_spec, k_spec, v_spec],
        out_specs=q_spec,
        compiler_params=_params("parallel", "arbitrary", "arbitrary"),
        name="diff_attn",
    )(slopes, lq1, lk1, lq2, lk2, subln_g, qkv, qkv, qkv)


def kernel(x, mix_pre_g, mix_post_g, ffn_pre_g, ffn_post_g, w_gate, w_up, w_down, a_w_qkv, a_w_o, a_sink, b_w_qkv, b_w_o, c_w_qkv, c_w_o, c_lambda_q1, c_lambda_k1, c_lambda_q2, c_lambda_k2, c_subln_g):
    bsz, seq, dm = x.shape
    n = bsz * seq
    slopes = _alibi_slopes()
    xf = x.reshape(n, dm)
    row = lambda a, i: a[i].reshape(1, -1)

    for i in range(DEPTH):
        m, j = i % N_MIXERS, i // N_MIXERS
        w_qkv = (a_w_qkv, b_w_qkv, c_w_qkv)[m][j].astype(BF16)
        w_o = (a_w_o, b_w_o, c_w_o)[m][j].astype(BF16)
        qkv = _norm_matmul(xf, row(mix_pre_g, i), w_qkv).reshape(bsz, seq, -1)
        lse = None
        if m == 0:
            o = jnp.zeros((bsz, seq, dm), BF16)
            (o,) = _band_attn(qkv, slopes, a_sink[j], q_col=0, k_col=N_HEADS, v_col=N_HEADS + N_KV_HEADS,
                              head0=0, n_heads=N_HEADS, kv_group=GQA_GROUP,
                              half_window=WINDOW_A, dilation=1, out_o=o)
        elif m == 1:
            o = jnp.zeros((bsz, seq, dm), BF16)
            lse = jnp.zeros((bsz, seq, dm), F32)
            h0 = 0
            for (window, dil), hg in zip(DILATED_PATTERNS, B_GROUP_HEADS):
                o, lse = _band_attn(qkv, slopes, None, q_col=0, k_col=N_HEADS, v_col=2 * N_HEADS,
                                    head0=h0, n_heads=hg, kv_group=1,
                                    half_window=window // 2, dilation=dil, out_o=o, out_lse=lse)
                h0 += hg
            lse = lse.reshape(n, dm)
        else:
            lambda_init = 0.8 - 0.6 * math.exp(-0.3 * i)
            o = _diff_attn(qkv, slopes, row(c_lambda_q1, j), row(c_lambda_k1, j), row(c_lambda_q2, j),
                           row(c_lambda_k2, j), row(c_subln_g, j), lambda_init)
        xf = _out_proj(o.reshape(n, dm), w_o, row(mix_post_g, i), xf, lse=lse)
        xf = _ffn(xf, row(ffn_pre_g, i), w_gate[i].astype(BF16), w_up[i].astype(BF16),
                  w_down[i].astype(BF16), row(ffn_post_g, i))
    return xf.reshape(bsz, seq, dm)
```

```python
import functools
import math

import jax
import jax.numpy as jnp
import ml_dtypes
import numpy as np
from jax import lax
from jax.experimental import pallas as pl
from jax.experimental.pallas import tpu as pltpu

F32 = jnp.float32
BF16 = jnp.bfloat16

D_MODEL = 2048
DEPTH = 4
N_MIXERS = 3
HEAD_DIM = 128
N_HEADS = 16
N_KV_HEADS = 4
GQA_GROUP = N_HEADS // N_KV_HEADS
DIFF_HEAD_DIM = HEAD_DIM // 2
WINDOW_A = 128
DILATED_PATTERNS = ((128, 1), (512, 4), (2048, 16))
B_GROUP_HEADS = (6, 5, 5)
FFN_HIDDEN = 5632
RMS_EPS = 1e-6
NEG_INF = -1e30
LOG2E = math.log2(math.e)

VMEM_LIMIT_BYTES = 48 * 1024 * 1024

PROJ_TM, PROJ_TN = 512, 1024
OUT_TM = 512
FFN_TM, FFN_TH = 512, 512
LOCAL_BQ = 128
LOCAL_GROUP = 8
DIFF_BQ = 128
DIFF_KC = 512
DIFF_UNROLL = 4
N_AUG = 3


def _alibi_slopes_np():
    return 2.0 ** (-8.0 * np.arange(1, N_HEADS + 1, dtype=np.float64) / N_HEADS)


def _rms(x, g):
    return x * lax.rsqrt(jnp.mean(x * x, axis=-1, keepdims=True) + RMS_EPS) * g


def _params(*sem):
    return pltpu.CompilerParams(dimension_semantics=sem, vmem_limit_bytes=VMEM_LIMIT_BYTES)


def _norm_matmul_kernel(x_ref, g_ref, w_ref, o_ref, h_ref, *, scaled_tiles, scale):
    j = pl.program_id(1)

    @pl.when(j == 0)
    def _():
        h_ref[...] = _rms(x_ref[...], g_ref[...]).astype(BF16)

    acc = jnp.dot(h_ref[...], w_ref[...], preferred_element_type=F32)
    if scaled_tiles:
        acc = acc * jnp.where(j < scaled_tiles, scale, 1.0)
    o_ref[...] = acc.astype(o_ref.dtype)


def _norm_matmul(x, g, w, layer, scaled_cols=0, scale=1.0):
    n, dm = x.shape
    m = w.shape[2]
    assert scaled_cols % PROJ_TN == 0
    return pl.pallas_call(
        functools.partial(_norm_matmul_kernel, scaled_tiles=scaled_cols // PROJ_TN, scale=scale),
        out_shape=jax.ShapeDtypeStruct((n, m), BF16),
        grid=(n // PROJ_TM, m // PROJ_TN),
        in_specs=[
            pl.BlockSpec((PROJ_TM, dm), lambda i, j: (i, 0)),
            pl.BlockSpec((1, dm), lambda i, j: (0, 0)),
            pl.BlockSpec((None, dm, PROJ_TN), lambda i, j: (layer, 0, j)),
        ],
        out_specs=pl.BlockSpec((PROJ_TM, PROJ_TN), lambda i, j: (i, j)),
        scratch_shapes=[pltpu.VMEM((PROJ_TM, dm), BF16)],
        compiler_params=_params("parallel", "arbitrary"),
        name="norm_matmul",
    )(x, g, w)


def _out_kernel(o_ref, w_ref, g_ref, x_ref, out_ref):
    y = jnp.dot(o_ref[...], w_ref[...], preferred_element_type=F32)
    out_ref[...] = x_ref[...] + _rms(y, g_ref[...])


def _out_merge_kernel(o_ref, lse_ref, w_ref, g_ref, x_ref, out_ref):
    scores = []
    h0 = 0
    for hg in B_GROUP_HEADS:
        acc = lse_ref[:, h0 * HEAD_DIM:(h0 + 1) * HEAD_DIM]
        for h in range(h0 + 1, h0 + hg):
            acc = acc + lse_ref[:, h * HEAD_DIM:(h + 1) * HEAD_DIM]
        scores.append(acc * (1.0 / hg))
        h0 += hg
    mx = jnp.maximum(jnp.maximum(scores[0], scores[1]), scores[2])
    es = [jnp.exp(s - mx) for s in scores]
    inv = 1.0 / (es[0] + es[1] + es[2])
    pieces = []
    h0 = 0
    for gi, hg in enumerate(B_GROUP_HEADS):
        alpha = es[gi] * inv
        for h in range(h0, h0 + hg):
            oh = o_ref[:, h * HEAD_DIM:(h + 1) * HEAD_DIM].astype(F32)
            pieces.append((oh * alpha).astype(BF16))
        h0 += hg
    mixed = jnp.concatenate(pieces, axis=-1)
    y = jnp.dot(mixed, w_ref[...], preferred_element_type=F32)
    out_ref[...] = x_ref[...] + _rms(y, g_ref[...])


def _out_proj(o, w, layer, g, x, lse=None):
    n, dm = x.shape
    row = lambda i: (i, 0)
    fixed = lambda i: (0, 0)
    o_spec = pl.BlockSpec((OUT_TM, dm), row)
    w_spec = pl.BlockSpec((None, dm, dm), lambda i: (layer, 0, 0))
    g_spec = pl.BlockSpec((1, dm), fixed)
    if lse is None:
        body, args = _out_kernel, (o, w, g, x)
        in_specs = [o_spec, w_spec, g_spec, o_spec]
    else:
        body, args = _out_merge_kernel, (o, lse, w, g, x)
        in_specs = [o_spec, o_spec, w_spec, g_spec, o_spec]
    return pl.pallas_call(
        body,
        out_shape=jax.ShapeDtypeStruct((n, dm), F32),
        grid=(n // OUT_TM,),
        in_specs=in_specs,
        out_specs=o_spec,
        compiler_params=_params("parallel"),
        name="out_proj",
    )(*args)


def _ffn_kernel(x_ref, gpre_ref, wg_ref, wu_ref, wd_ref, gpost_ref, out_ref, h_ref, acc_ref):
    k = pl.program_id(1)

    @pl.when(k == 0)
    def _():
        h_ref[...] = _rms(x_ref[...], gpre_ref[...]).astype(BF16)

    h = h_ref[...]
    gate = jnp.dot(h, wg_ref[...], preferred_element_type=F32)
    up = jnp.dot(h, wu_ref[...], preferred_element_type=F32)
    act = (gate * jax.nn.sigmoid(gate) * up).astype(BF16)
    part = jnp.dot(act, wd_ref[...], preferred_element_type=F32)

    @pl.when(k == 0)
    def _():
        acc_ref[...] = part

    @pl.when(k > 0)
    def _():
        acc_ref[...] += part

    @pl.when(k == pl.num_programs(1) - 1)
    def _():
        out_ref[...] = x_ref[...] + _rms(acc_ref[...], gpost_ref[...])


def _ffn(x, gpre, wg, wu, wd, gpost, layer):
    n, dm = x.shape
    hid = wg.shape[2]
    row = lambda i, k: (i, 0)
    fixed = lambda i, k: (0, 0)
    return pl.pallas_call(
        _ffn_kernel,
        out_shape=jax.ShapeDtypeStruct((n, dm), F32),
        grid=(n // FFN_TM, hid // FFN_TH),
        in_specs=[
            pl.BlockSpec((FFN_TM, dm), row),
            pl.BlockSpec((1, dm), fixed),
            pl.BlockSpec((None, dm, FFN_TH), lambda i, k: (layer, 0, k)),
            pl.BlockSpec((None, dm, FFN_TH), lambda i, k: (layer, 0, k)),
            pl.BlockSpec((None, FFN_TH, dm), lambda i, k: (layer, k, 0)),
            pl.BlockSpec((1, dm), fixed),
        ],
        out_specs=pl.BlockSpec((FFN_TM, dm), row),
        scratch_shapes=[pltpu.VMEM((FFN_TM, dm), BF16), pltpu.VMEM((FFN_TM, dm), F32)],
        compiler_params=_params("parallel", "arbitrary"),
        name="ffn",
    )(x, gpre, wg, wu, wd, gpost)


def _local_attn_body(slope, sink, q_ref, k_ref, v_ref, o_ref, lse_ref, bm_ref, scratch, *, half, dil):
    seq = q_ref.shape[0]
    bq = LOCAL_BQ
    lc = seq // dil
    ks = min(3 * bq, lc)
    nqb = lc // bq
    if dil > 1:
        qf, kf, vf, of = scratch
        qf[...] = q_ref[...].astype(F32)
        kf[...] = k_ref[...].astype(F32)
        vf[...] = v_ref[...].astype(F32)
    rel0 = (lax.broadcasted_iota(jnp.int32, (bq, ks), 0)
            - lax.broadcasted_iota(jnp.int32, (bq, ks), 1))
    step_slope = slope * (dil * LOG2E)
    for case in range(3):
        dist = jnp.abs(rel0 + case * bq)
        bm_ref[case, :, :ks] = jnp.where(dist <= half, -step_slope * dist.astype(F32), NEG_INF)
    sink2 = None if sink is None else sink * LOG2E

    def logits(idx):
        r = idx // nqb
        q0 = (idx % nqb) * bq
        k0 = jnp.clip(q0 - bq, 0, lc - ks)
        if dil == 1:
            rows = pl.ds(pl.multiple_of(q0, bq), bq)
            q = q_ref[rows, :]
            k = k_ref[pl.ds(pl.multiple_of(k0, bq), ks), :]
            v = v_ref[pl.ds(pl.multiple_of(k0, bq), ks), :]
        else:
            rows = pl.ds(r + dil * q0, bq, stride=dil)
            q = qf[rows, :].astype(BF16)
            k = kf[pl.ds(r + dil * k0, ks, stride=dil), :].astype(BF16)
            v = vf[pl.ds(r + dil * k0, ks, stride=dil), :].astype(BF16)
        s = lax.dot_general(q, k, (((1,), (1,)), ((), ())), preferred_element_type=F32)
        return s + bm_ref[(q0 - k0) // bq, :, :ks], v, rows

    def softmax(s):
        mx = jnp.max(s, axis=-1, keepdims=True)
        if sink2 is not None:
            mx = jnp.maximum(mx, sink2)
        e = jnp.exp2(s - mx)
        den = jnp.sum(e, axis=-1, keepdims=True)
        if sink2 is not None:
            den = den + jnp.exp2(sink2 - mx)
        return e.astype(BF16), mx, den

    def finish(e, mx, den, v, rows):
        o = jnp.dot(e, v, preferred_element_type=F32) / den
        if dil == 1:
            o_ref[rows, :] = o.astype(o_ref.dtype)
        else:
            of[rows, :] = o
        if lse_ref is not None:
            lse = mx * (1.0 / LOG2E) + jnp.log(den)
            lse_ref[rows, :] = jnp.broadcast_to(lse, (bq, HEAD_DIM))

    def group(gidx, carry):
        staged = [logits(gidx * LOCAL_GROUP + u) for u in range(LOCAL_GROUP)]
        probs = [softmax(s) for s, _, _ in staged]
        for (e, mx, den), (_, v, rows) in zip(probs, staged):
            finish(e, mx, den, v, rows)
        return carry

    lax.fori_loop(0, dil * nqb // LOCAL_GROUP, group, 0)
    if dil > 1:
        o_ref[...] = of[...].astype(o_ref.dtype)


def _window_attn_kernel(slope_ref, sink_ref, q_ref, k_ref, v_ref, o_ref, bm_ref):
    head = pl.program_id(1)
    _local_attn_body(slope_ref[head], sink_ref[head], q_ref, k_ref, v_ref, o_ref, None, bm_ref, None,
                     half=WINDOW_A, dil=1)


def _dilated_attn_kernel(slope_ref, q_ref, k_ref, v_ref, o_ref, lse_ref, bm_ref, *scratch):
    head = pl.program_id(1)
    h0 = 0
    for (window, dil), hg in zip(DILATED_PATTERNS, B_GROUP_HEADS):
        @pl.when((head >= h0) & (head < h0 + hg))
        def _(window=window, dil=dil):
            _local_attn_body(slope_ref[head], None, q_ref, k_ref, v_ref, o_ref, lse_ref, bm_ref, scratch,
                             half=window // (2 * dil), dil=dil)
        h0 += hg


_BIAS_TILES = pltpu.VMEM((3, LOCAL_BQ, 3 * LOCAL_BQ), F32)


def _head_spec(seq, col0, group=1):
    return pl.BlockSpec((None, seq, HEAD_DIM), lambda b, h: (b, 0, col0 + h // group))


def _window_attn(qkv, slopes, sink):
    bsz, seq, _ = qkv.shape
    smem = pl.BlockSpec(memory_space=pltpu.SMEM)
    return pl.pallas_call(
        _window_attn_kernel,
        out_shape=jax.ShapeDtypeStruct((bsz, seq, N_HEADS * HEAD_DIM), BF16),
        grid=(bsz, N_HEADS),
        in_specs=[smem, smem, _head_spec(seq, 0), _head_spec(seq, N_HEADS, GQA_GROUP),
                  _head_spec(seq, N_HEADS + N_KV_HEADS, GQA_GROUP)],
        out_specs=_head_spec(seq, 0),
        scratch_shapes=[_BIAS_TILES],
        compiler_params=_params("parallel", "arbitrary"),
        name="window_attn",
    )(slopes, sink, qkv, qkv, qkv)


def _dilated_attn(qkv, slopes):
    bsz, seq, _ = qkv.shape
    width = N_HEADS * HEAD_DIM
    return pl.pallas_call(
        _dilated_attn_kernel,
        out_shape=[jax.ShapeDtypeStruct((bsz, seq, width), BF16),
                   jax.ShapeDtypeStruct((bsz, seq, width), F32)],
        grid=(bsz, N_HEADS),
        in_specs=[pl.BlockSpec(memory_space=pltpu.SMEM), _head_spec(seq, 0),
                  _head_spec(seq, N_HEADS), _head_spec(seq, 2 * N_HEADS)],
        out_specs=[_head_spec(seq, 0), _head_spec(seq, 0)],
        scratch_shapes=[_BIAS_TILES] + [pltpu.VMEM((seq, HEAD_DIM), F32) for _ in range(4)],
        compiler_params=_params("parallel", "arbitrary"),
        name="dilated_attn",
    )(slopes, qkv, qkv, qkv)


AUG_SIGN, AUG_HI, AUG_LO, AUG_MASK = 0, N_AUG, 2 * N_AUG, 3 * N_AUG


def _split_bf16(x, n=N_AUG):
    terms, rest = [], np.asarray(x, np.float64)
    for _ in range(n):
        t = rest.astype(ml_dtypes.bfloat16).astype(np.float64)
        terms.append(t)
        rest = rest - t
    return terms


def _diff_tables(seq):
    bq = DIFF_BQ
    m2 = _alibi_slopes_np() * LOG2E
    t = np.arange(bq, dtype=np.float64)
    qaug = np.zeros((N_KV_HEADS, 2, GQA_GROUP, bq, HEAD_DIM), np.float64)
    dbias = np.zeros((N_KV_HEADS, 2, GQA_GROUP, bq, bq), np.float64)
    for kvh in range(N_KV_HEADS):
        for g in range(GQA_GROUP):
            m = m2[kvh * GQA_GROUP + g]
            for i, term in enumerate(_split_bf16(m * t)):
                qaug[kvh, :, g, :, AUG_SIGN + i] = term
            for i, term in enumerate(_split_bf16(-float(bq) * m)):
                qaug[kvh, :, g, :, AUG_HI + i] = term
            for i, term in enumerate(_split_bf16(-m)):
                qaug[kvh, :, g, :, AUG_LO + i] = term
            qaug[kvh, :, g, :, AUG_MASK] = NEG_INF
            dbias[kvh, :, g] = -m * np.abs(t[:, None] - t[None, :])
    rows = 2 * GQA_GROUP * bq
    qaug = qaug.reshape(N_KV_HEADS, rows, HEAD_DIM).astype(ml_dtypes.bfloat16)
    dbias = dbias.reshape(N_KV_HEADS, rows, bq).astype(np.float32)
    rel = np.arange(-seq, seq, dtype=np.int64)
    hi = np.floor_divide(rel, bq)
    lo = rel - hi * bq
    sign = np.where(rel >= bq, 1, np.where(rel < 0, -1, 0))
    kaug = np.zeros((2 * seq, HEAD_DIM), np.float64)
    for i in range(N_AUG):
        kaug[:, AUG_SIGN + i] = sign
        kaug[:, AUG_HI + i] = sign * hi
        kaug[:, AUG_LO + i] = sign * lo
    kaug[:, AUG_MASK] = sign == 0
    return jnp.asarray(qaug), jnp.asarray(dbias), jnp.asarray(kaug.astype(ml_dtypes.bfloat16))


def _lane_tiles(x):
    return [x[:, j * HEAD_DIM:(j + 1) * HEAD_DIM] for j in range(x.shape[1] // HEAD_DIM)]


def _diff_attn_kernel(lq1_ref, lk1_ref, lq2_ref, lk2_ref, sg_ref, qaug_ref, dbias_ref, kaug_ref,
                      q_ref, k_ref, v_ref, o_ref,
                      kp_ref, qp_ref, l_ref, ld_ref, mx_ref, den_ref, acc_ref, *, lambda_init):
    bq = DIFF_BQ
    seq = k_ref.shape[0]
    nchunk = seq // DIFF_KC
    n = pl.program_id(2)
    t0 = pl.multiple_of(n * bq, bq)
    dn = (((1,), (1,)), ((), ()))

    @pl.when(n == 0)
    def _():
        kp_ref[:, :HEAD_DIM] = k_ref[...]

    kp_ref[:, HEAD_DIM:] = kaug_ref[pl.ds(pl.multiple_of(seq - t0, bq), seq), :]

    lane = lax.broadcasted_iota(jnp.int32, (bq, HEAD_DIM), 1)
    for c in range(2):
        keep = (lane < DIFF_HEAD_DIM) if c == 0 else (lane >= DIFF_HEAD_DIM)
        for g in range(GQA_GROUP):
            qg = q_ref[:, g * HEAD_DIM:(g + 1) * HEAD_DIM]
            r0 = (c * GQA_GROUP + g) * bq
            qp_ref[r0:r0 + bq, :HEAD_DIM] = jnp.where(keep, qg, jnp.zeros_like(qg))
    qp_ref[:, HEAD_DIM:] = qaug_ref[...]

    ld = lax.dot_general(qp_ref[:, :HEAD_DIM], k_ref[pl.ds(t0, bq), :], dn,
                         preferred_element_type=F32) + dbias_ref[...]
    ld_ref[...] = ld
    mx_ref[...] = ld

    def qk_chunk(c, carry):
        kc = kp_ref[pl.ds(pl.multiple_of(c * DIFF_KC, DIFF_KC), DIFF_KC), :]
        l = lax.dot_general(qp_ref[...], kc, dn, preferred_element_type=F32)
        l_ref[c] = l
        m = mx_ref[...]
        for lt in _lane_tiles(l):
            m = jnp.maximum(m, lt)
        mx_ref[...] = m
        return carry

    lax.fori_loop(0, nchunk, qk_chunk, 0, unroll=DIFF_UNROLL)
    mx_ref[...] = jnp.broadcast_to(jnp.max(mx_ref[...], axis=-1, keepdims=True), mx_ref.shape)

    ed = jnp.exp2(ld_ref[...] - mx_ref[...])
    den_ref[...] = ed
    acc_ref[...] = jnp.dot(ed.astype(BF16), v_ref[pl.ds(t0, bq), :], preferred_element_type=F32)

    def pv_chunk(c, carry):
        mxb = mx_ref[...]
        es = [jnp.exp2(lt - mxb) for lt in _lane_tiles(l_ref[c])]
        den_ref[...] += (es[0] + es[1]) + (es[2] + es[3])
        e = jnp.concatenate([x.astype(BF16) for x in es], axis=-1)
        vc = v_ref[pl.ds(pl.multiple_of(c * DIFF_KC, DIFF_KC), DIFF_KC), :]
        acc_ref[...] += jnp.dot(e, vc, preferred_element_type=F32)
        return carry

    lax.fori_loop(0, nchunk, pv_chunk, 0, unroll=DIFF_UNROLL)

    lam = (jnp.exp(jnp.sum(lq1_ref[...] * lk1_ref[...], axis=-1, keepdims=True))
           - jnp.exp(jnp.sum(lq2_ref[...] * lk2_ref[...], axis=-1, keepdims=True))
           + lambda_init)
    p = acc_ref[...] / jnp.sum(den_ref[...], axis=-1, keepdims=True)
    half_rows = GQA_GROUP * bq
    o = p[:half_rows] - lam * p[half_rows:]
    for g in range(GQA_GROUP):
        og = _rms(o[g * bq:(g + 1) * bq], sg_ref[...]) * (1.0 - lambda_init)
        o_ref[:, g * HEAD_DIM:(g + 1) * HEAD_DIM] = og.astype(o_ref.dtype)


def _diff_attn(qkv, lq1, lk1, lq2, lk2, subln_g, lambda_init):
    bsz, seq, _ = qkv.shape
    assert DIFF_KC == 4 * HEAD_DIM and seq % DIFF_KC == 0
    bq = DIFF_BQ
    rows = 2 * GQA_GROUP * bq
    gw = GQA_GROUP * HEAD_DIM
    qaug, dbias, kaug = _diff_tables(seq)
    vec = lambda w: pl.BlockSpec((1, w), lambda b, h, n: (0, 0))
    per_head = lambda w: pl.BlockSpec((None, rows, w), lambda b, h, n: (h, 0, 0))
    q_spec = pl.BlockSpec((None, bq, gw), lambda b, h, n: (b, n, h))
    k_spec = pl.BlockSpec((None, seq, HEAD_DIM), lambda b, h, n: (b, 0, N_HEADS + h))
    v_spec = pl.BlockSpec((None, seq, HEAD_DIM), lambda b, h, n: (b, 0, N_HEADS + N_KV_HEADS + h))
    return pl.pallas_call(
        functools.partial(_diff_attn_kernel, lambda_init=lambda_init),
        out_shape=jax.ShapeDtypeStruct((bsz, seq, N_HEADS * HEAD_DIM), BF16),
        grid=(bsz, N_KV_HEADS, seq // bq),
        in_specs=[vec(DIFF_HEAD_DIM), vec(DIFF_HEAD_DIM), vec(DIFF_HEAD_DIM), vec(DIFF_HEAD_DIM),
                  vec(HEAD_DIM), per_head(HEAD_DIM), per_head(bq),
                  pl.BlockSpec((2 * seq, HEAD_DIM), lambda b, h, n: (0, 0)),
                  q_spec, k_spec, v_spec],
        out_specs=q_spec,
        scratch_shapes=[
            pltpu.VMEM((seq, 2 * HEAD_DIM), BF16),
            pltpu.VMEM((rows, 2 * HEAD_DIM), BF16),
            pltpu.VMEM((seq // DIFF_KC, rows, DIFF_KC), F32),
            pltpu.VMEM((rows, bq), F32),
            pltpu.VMEM((rows, HEAD_DIM), F32),
            pltpu.VMEM((rows, HEAD_DIM), F32),
            pltpu.VMEM((rows, HEAD_DIM), F32),
        ],
        compiler_params=_params("parallel", "arbitrary", "arbitrary"),
        name="diff_attn",
    )(lq1, lk1, lq2, lk2, subln_g, qaug, dbias, kaug, qkv, qkv, qkv)


def kernel(x, mix_pre_g, mix_post_g, ffn_pre_g, ffn_post_g, w_gate, w_up, w_down, a_w_qkv, a_w_o, a_sink, b_w_qkv, b_w_o, c_w_qkv, c_w_o, c_lambda_q1, c_lambda_k1, c_lambda_q2, c_lambda_k2, c_subln_g):
    bsz, seq, dm = x.shape
    n = bsz * seq
    slopes = jnp.asarray(_alibi_slopes_np(), F32)
    xf = x.reshape(n, dm)
    row = lambda a, i: a[i].reshape(1, -1)
    w_qkv = [w.astype(BF16) for w in (a_w_qkv, b_w_qkv, c_w_qkv)]
    w_o = [w.astype(BF16) for w in (a_w_o, b_w_o, c_w_o)]
    wg, wu, wd = w_gate.astype(BF16), w_up.astype(BF16), w_down.astype(BF16)

    for i in range(DEPTH):
        m, j = i % N_MIXERS, i // N_MIXERS
        pre = row(mix_pre_g, i)
        lse = None
        q_cols = N_HEADS * HEAD_DIM
        if m == 0:
            qkv = _norm_matmul(xf, pre, w_qkv[m], j, q_cols, LOG2E * HEAD_DIM ** -0.5).reshape(bsz, seq, -1)
            o = _window_attn(qkv, slopes, a_sink[j])
        elif m == 1:
            qkv = _norm_matmul(xf, pre, w_qkv[m], j, q_cols, LOG2E * HEAD_DIM ** -0.5).reshape(bsz, seq, -1)
            o, lse = _dilated_attn(qkv, slopes)
            lse = lse.reshape(n, dm)
        else:
            lambda_init = 0.8 - 0.6 * math.exp(-0.3 * i)
            qkv = _norm_matmul(xf, pre, w_qkv[m], j, q_cols, LOG2E * DIFF_HEAD_DIM ** -0.5).reshape(bsz, seq, -1)
            o = _diff_attn(qkv, row(c_lambda_q1, j), row(c_lambda_k1, j), row(c_lambda_q2, j),
                           row(c_lambda_k2, j), row(c_subln_g, j), lambda_init)
        xf = _out_proj(o.reshape(n, dm), w_o[m], j, row(mix_post_g, i), xf, lse=lse)
        xf = _ffn(xf, row(ffn_pre_g, i), wg, wu, wd, row(ffn_post_g, i), i)
    return xf.reshape(bsz, seq, dm)
```

```python
import functools
import math

import jax
import jax.numpy as jnp
import ml_dtypes
import numpy as np
from jax import lax
from jax.experimental import pallas as pl
from jax.experimental.pallas import tpu as pltpu

F32 = jnp.float32
BF16 = jnp.bfloat16

D_MODEL = 2048
DEPTH = 4
N_MIXERS = 3
HEAD_DIM = 128
N_HEADS = 16
N_KV_HEADS = 4
GQA_GROUP = N_HEADS // N_KV_HEADS
DIFF_HEAD_DIM = HEAD_DIM // 2
WINDOW_A = 128
DILATED_PATTERNS = ((128, 1), (512, 4), (2048, 16))
B_GROUP_HEADS = (6, 5, 5)
FFN_HIDDEN = 5632
RMS_EPS = 1e-6
NEG_INF = -1e30
LOG2E = math.log2(math.e)

VMEM_LIMIT_BYTES = 48 * 1024 * 1024
FFN_VMEM_LIMIT_BYTES = 56 * 1024 * 1024

PROJ_TM, PROJ_TN = 1024, 1024
OUT_TM = 512
FFN_TM, FFN_TH = 512, 512
LOCAL_BQ = 128
LOCAL_GROUP = 8
DIFF_BQ = 128
DIFF_KC = 512
DIFF_UNROLL = 4
N_AUG = 3


def _alibi_slopes_np():
    return 2.0 ** (-8.0 * np.arange(1, N_HEADS + 1, dtype=np.float64) / N_HEADS)


def _rms(x, g):
    return x * lax.rsqrt(jnp.mean(x * x, axis=-1, keepdims=True) + RMS_EPS) * g


def _params(*sem, vmem_limit_bytes=VMEM_LIMIT_BYTES):
    return pltpu.CompilerParams(dimension_semantics=sem, vmem_limit_bytes=vmem_limit_bytes)


def _norm_matmul_kernel(x_ref, g_ref, w_ref, o_ref, h_ref, *, scaled_tiles, scale):
    j = pl.program_id(1)

    @pl.when(j == 0)
    def _():
        h_ref[...] = _rms(x_ref[...], g_ref[...]).astype(BF16)

    acc = jnp.dot(h_ref[...], w_ref[...], preferred_element_type=F32)
    if scaled_tiles:
        acc = acc * jnp.where(j < scaled_tiles, scale, 1.0)
    o_ref[...] = acc.astype(o_ref.dtype)


def _norm_matmul(x, g, w, layer, scaled_cols=0, scale=1.0):
    n, dm = x.shape
    m = w.shape[2]
    assert scaled_cols % PROJ_TN == 0
    return pl.pallas_call(
        functools.partial(_norm_matmul_kernel, scaled_tiles=scaled_cols // PROJ_TN, scale=scale),
        out_shape=jax.ShapeDtypeStruct((n, m), BF16),
        grid=(n // PROJ_TM, m // PROJ_TN),
        in_specs=[
            pl.BlockSpec((PROJ_TM, dm), lambda i, j: (i, 0)),
            pl.BlockSpec((1, dm), lambda i, j: (0, 0)),
            pl.BlockSpec((None, dm, PROJ_TN), lambda i, j: (layer, 0, j)),
        ],
        out_specs=pl.BlockSpec((PROJ_TM, PROJ_TN), lambda i, j: (i, j)),
        scratch_shapes=[pltpu.VMEM((PROJ_TM, dm), BF16)],
        compiler_params=_params("parallel", "arbitrary"),
        name="norm_matmul",
    )(x, g, w)


def _out_kernel(o_ref, w_ref, g_ref, x_ref, out_ref):
    y = jnp.dot(o_ref[...], w_ref[...], preferred_element_type=F32)
    out_ref[...] = x_ref[...] + _rms(y, g_ref[...])


def _out_merge_kernel(o_ref, lse_ref, w_ref, g_ref, x_ref, out_ref):
    scores = []
    h0 = 0
    for hg in B_GROUP_HEADS:
        acc = lse_ref[:, h0 * HEAD_DIM:(h0 + 1) * HEAD_DIM]
        for h in range(h0 + 1, h0 + hg):
            acc = acc + lse_ref[:, h * HEAD_DIM:(h + 1) * HEAD_DIM]
        scores.append(acc * (1.0 / hg))
        h0 += hg
    mx = jnp.maximum(jnp.maximum(scores[0], scores[1]), scores[2])
    es = [jnp.exp(s - mx) for s in scores]
    inv = 1.0 / (es[0] + es[1] + es[2])
    pieces = []
    h0 = 0
    for gi, hg in enumerate(B_GROUP_HEADS):
        alpha = es[gi] * inv
        for h in range(h0, h0 + hg):
            oh = o_ref[:, h * HEAD_DIM:(h + 1) * HEAD_DIM].astype(F32)
            pieces.append((oh * alpha).astype(BF16))
        h0 += hg
    mixed = jnp.concatenate(pieces, axis=-1)
    y = jnp.dot(mixed, w_ref[...], preferred_element_type=F32)
    out_ref[...] = x_ref[...] + _rms(y, g_ref[...])


def _out_proj(o, w, layer, g, x, lse=None):
    n, dm = x.shape
    row = lambda i: (i, 0)
    fixed = lambda i: (0, 0)
    o_spec = pl.BlockSpec((OUT_TM, dm), row)
    w_spec = pl.BlockSpec((None, dm, dm), lambda i: (layer, 0, 0))
    g_spec = pl.BlockSpec((1, dm), fixed)
    if lse is None:
        body, args = _out_kernel, (o, w, g, x)
        in_specs = [o_spec, w_spec, g_spec, o_spec]
    else:
        body, args = _out_merge_kernel, (o, lse, w, g, x)
        in_specs = [o_spec, o_spec, w_spec, g_spec, o_spec]
    return pl.pallas_call(
        body,
        out_shape=jax.ShapeDtypeStruct((n, dm), F32),
        grid=(n // OUT_TM,),
        in_specs=in_specs,
        out_specs=o_spec,
        compiler_params=_params("parallel"),
        name="out_proj",
    )(*args)


def _ffn_kernel(x_ref, gpre_ref, wg_ref, wu_ref, wd_ref, gpost_ref, *rest, n_tile_casts, n_slab_casts):
    n_casts = n_tile_casts + n_slab_casts
    cast_in, out_ref, cast_out = rest[:n_casts], rest[n_casts], rest[n_casts + 1:2 * n_casts + 1]
    h_ref, acc_ref = rest[2 * n_casts + 1:]
    k = pl.program_id(1)

    for src, dst in zip(cast_in[:n_tile_casts], cast_out[:n_tile_casts]):
        dst[...] = src[...].astype(BF16)

    @pl.when(k == 0)
    def _():
        h_ref[...] = _rms(x_ref[...], gpre_ref[...]).astype(BF16)
        acc_ref[...] = jnp.zeros_like(acc_ref)
        for src, dst in zip(cast_in[n_tile_casts:], cast_out[n_tile_casts:]):
            dst[...] = src[...].astype(BF16)

    h = h_ref[...]
    gate = jnp.dot(h, wg_ref[...], preferred_element_type=F32)
    up = jnp.dot(h, wu_ref[...], preferred_element_type=F32)
    act = (gate * jax.nn.sigmoid(gate) * up).astype(BF16)
    acc_ref[...] += jnp.dot(act, wd_ref[...], preferred_element_type=F32)

    @pl.when(k == pl.num_programs(1) - 1)
    def _():
        out_ref[...] = x_ref[...] + _rms(acc_ref[...], gpost_ref[...])


def _ffn(x, gpre, wg, wu, wd, gpost, next_ffn=None, slab_casts=()):
    n, dm = x.shape
    hid = wg.shape[2]
    grid = (n // FFN_TM, hid // FFN_TH)
    row = lambda i, k: (i, 0)
    fixed = lambda i, k: (0, 0)
    in_specs = [
        pl.BlockSpec((FFN_TM, dm), row),
        pl.BlockSpec((1, dm), fixed),
        pl.BlockSpec((None, dm, FFN_TH), lambda i, k: (0, 0, k)),
        pl.BlockSpec((None, dm, FFN_TH), lambda i, k: (0, 0, k)),
        pl.BlockSpec((None, FFN_TH, dm), lambda i, k: (0, k, 0)),
        pl.BlockSpec((1, dm), fixed),
    ]
    args = [x, gpre, wg, wu, wd, gpost]
    out_shape = [jax.ShapeDtypeStruct((n, dm), F32)]
    out_specs = [pl.BlockSpec((FFN_TM, dm), row)]

    def add_cast(stack, block, in_map, out_map):
        in_specs.append(pl.BlockSpec((None,) + block, in_map))
        args.append(stack)
        out_shape.append(jax.ShapeDtypeStruct((1,) + stack.shape[1:], BF16))
        out_specs.append(pl.BlockSpec((None,) + block, out_map))

    n_tile_casts = 0
    if next_ffn is not None:
        ng, nu, nd, layer = next_ffn
        cr = dm // grid[0]
        assert dm % grid[0] == 0 and cr % 8 == 0
        add_cast(ng, (cr, FFN_TH), lambda i, k: (layer, i, k), lambda i, k: (0, i, k))
        add_cast(nu, (cr, FFN_TH), lambda i, k: (layer, i, k), lambda i, k: (0, i, k))
        add_cast(nd, (FFN_TH, cr), lambda i, k: (layer, k, i), lambda i, k: (0, k, i))
        n_tile_casts = 3
    for stack, index in slab_casts:
        rows = stack.shape[1] // grid[0]
        assert stack.shape[1] % grid[0] == 0 and rows % 8 == 0
        add_cast(stack, (rows, stack.shape[2]),
                 lambda i, k, index=index: (index, i, 0), lambda i, k: (0, i, 0))
    res = pl.pallas_call(
        functools.partial(_ffn_kernel, n_tile_casts=n_tile_casts, n_slab_casts=len(slab_casts)),
        out_shape=out_shape,
        grid=grid,
        in_specs=in_specs,
        out_specs=out_specs,
        scratch_shapes=[pltpu.VMEM((FFN_TM, dm), BF16), pltpu.VMEM((FFN_TM, dm), F32)],
        compiler_params=_params("parallel", "arbitrary", vmem_limit_bytes=FFN_VMEM_LIMIT_BYTES),
        name="ffn",
    )(*args)
    return res[0], tuple(res[1:1 + n_tile_casts]), tuple(res[1 + n_tile_casts:])


def _local_attn_body(slope, sink, q_ref, k_ref, v_ref, o_ref, lse_ref, bm_ref, scratch, *, half, dil):
    seq = q_ref.shape[0]
    bq = LOCAL_BQ
    lc = seq // dil
    ks = min(3 * bq, lc)
    nqb = lc // bq
    if dil > 1:
        qf, kf, vf, of = scratch
        qf[...] = q_ref[...].astype(F32)
        kf[...] = k_ref[...].astype(F32)
        vf[...] = v_ref[...].astype(F32)
    rel0 = (lax.broadcasted_iota(jnp.int32, (bq, ks), 0)
            - lax.broadcasted_iota(jnp.int32, (bq, ks), 1))
    step_slope = slope * (dil * LOG2E)
    for case in range(3):
        dist = jnp.abs(rel0 + case * bq)
        bm_ref[case, :, :ks] = jnp.where(dist <= half, -step_slope * dist.astype(F32), NEG_INF)
    sink2 = None if sink is None else sink * LOG2E

    def logits(idx):
        r = idx // nqb
        q0 = (idx % nqb) * bq
        k0 = jnp.clip(q0 - bq, 0, lc - ks)
        if dil == 1:
            rows = pl.ds(pl.multiple_of(q0, bq), bq)
            q = q_ref[rows, :]
            k = k_ref[pl.ds(pl.multiple_of(k0, bq), ks), :]
            v = v_ref[pl.ds(pl.multiple_of(k0, bq), ks), :]
        else:
            rows = pl.ds(r + dil * q0, bq, stride=dil)
            q = qf[rows, :].astype(BF16)
            k = kf[pl.ds(r + dil * k0, ks, stride=dil), :].astype(BF16)
            v = vf[pl.ds(r + dil * k0, ks, stride=dil), :].astype(BF16)
        s = lax.dot_general(q, k, (((1,), (1,)), ((), ())), preferred_element_type=F32)
        return s + bm_ref[(q0 - k0) // bq, :, :ks], v, rows

    def softmax(s):
        mx = jnp.max(s, axis=-1, keepdims=True)
        if sink2 is not None:
            mx = jnp.maximum(mx, sink2)
        e = jnp.exp2(s - mx)
        den = jnp.sum(e, axis=-1, keepdims=True)
        if sink2 is not None:
            den = den + jnp.exp2(sink2 - mx)
        return e.astype(BF16), mx, den

    def finish(e, mx, den, v, rows):
        o = jnp.dot(e, v, preferred_element_type=F32) / den
        if dil == 1:
            o_ref[rows, :] = o.astype(o_ref.dtype)
        else:
            of[rows, :] = o
        if lse_ref is not None:
            lse = mx * (1.0 / LOG2E) + jnp.log(den)
            lse_ref[rows, :] = jnp.broadcast_to(lse, (bq, HEAD_DIM))

    def group(gidx, carry):
        staged = [logits(gidx * LOCAL_GROUP + u) for u in range(LOCAL_GROUP)]
        probs = [softmax(s) for s, _, _ in staged]
        for (e, mx, den), (_, v, rows) in zip(probs, staged):
            finish(e, mx, den, v, rows)
        return carry

    lax.fori_loop(0, dil * nqb // LOCAL_GROUP, group, 0)
    if dil > 1:
        o_ref[...] = of[...].astype(o_ref.dtype)


def _window_attn_kernel(slope_ref, sink_ref, q_ref, k_ref, v_ref, o_ref, bm_ref):
    head = pl.program_id(1)
    _local_attn_body(slope_ref[head], sink_ref[head], q_ref, k_ref, v_ref, o_ref, None, bm_ref, None,
                     half=WINDOW_A, dil=1)


def _dilated_attn_kernel(slope_ref, q_ref, k_ref, v_ref, o_ref, lse_ref, bm_ref, *scratch):
    head = pl.program_id(1)
    h0 = 0
    for (window, dil), hg in zip(DILATED_PATTERNS, B_GROUP_HEADS):
        @pl.when((head >= h0) & (head < h0 + hg))
        def _(window=window, dil=dil):
            _local_attn_body(slope_ref[head], None, q_ref, k_ref, v_ref, o_ref, lse_ref, bm_ref, scratch,
                             half=window // (2 * dil), dil=dil)
        h0 += hg


_BIAS_TILES = pltpu.VMEM((3, LOCAL_BQ, 3 * LOCAL_BQ), F32)


def _head_spec(seq, col0, group=1):
    return pl.BlockSpec((None, seq, HEAD_DIM), lambda b, h: (b, 0, col0 + h // group))


def _window_attn(qkv, slopes, sink):
    bsz, seq, _ = qkv.shape
    smem = pl.BlockSpec(memory_space=pltpu.SMEM)
    return pl.pallas_call(
        _window_attn_kernel,
        out_shape=jax.ShapeDtypeStruct((bsz, seq, N_HEADS * HEAD_DIM), BF16),
        grid=(bsz, N_HEADS),
        in_specs=[smem, smem, _head_spec(seq, 0), _head_spec(seq, N_HEADS, GQA_GROUP),
                  _head_spec(seq, N_HEADS + N_KV_HEADS, GQA_GROUP)],
        out_specs=_head_spec(seq, 0),
        scratch_shapes=[_BIAS_TILES],
        compiler_params=_params("parallel", "arbitrary"),
        name="window_attn",
    )(slopes, sink, qkv, qkv, qkv)


def _dilated_attn(qkv, slopes):
    bsz, seq, _ = qkv.shape
    width = N_HEADS * HEAD_DIM
    return pl.pallas_call(
        _dilated_attn_kernel,
        out_shape=[jax.ShapeDtypeStruct((bsz, seq, width), BF16),
                   jax.ShapeDtypeStruct((bsz, seq, width), F32)],
        grid=(bsz, N_HEADS),
        in_specs=[pl.BlockSpec(memory_space=pltpu.SMEM), _head_spec(seq, 0),
                  _head_spec(seq, N_HEADS), _head_spec(seq, 2 * N_HEADS)],
        out_specs=[_head_spec(seq, 0), _head_spec(seq, 0)],
        scratch_shapes=[_BIAS_TILES] + [pltpu.VMEM((seq, HEAD_DIM), F32) for _ in range(4)],
        compiler_params=_params("parallel", "arbitrary"),
        name="dilated_attn",
    )(slopes, qkv, qkv, qkv)


AUG_SIGN, AUG_HI, AUG_LO, AUG_MASK = 0, N_AUG, 2 * N_AUG, 3 * N_AUG


def _split_bf16(x, n=N_AUG):
    terms, rest = [], np.asarray(x, np.float64)
    for _ in range(n):
        t = rest.astype(ml_dtypes.bfloat16).astype(np.float64)
        terms.append(t)
        rest = rest - t
    return terms


def _diff_tables(seq):
    bq = DIFF_BQ
    m2 = _alibi_slopes_np() * LOG2E
    t = np.arange(bq, dtype=np.float64)
    qaug = np.zeros((N_KV_HEADS, 2, GQA_GROUP, bq, HEAD_DIM), np.float64)
    dbias = np.zeros((N_KV_HEADS, 2, GQA_GROUP, bq, bq), np.float64)
    for kvh in range(N_KV_HEADS):
        for g in range(GQA_GROUP):
            m = m2[kvh * GQA_GROUP + g]
            for i, term in enumerate(_split_bf16(m * t)):
                qaug[kvh, :, g, :, AUG_SIGN + i] = term
            for i, term in enumerate(_split_bf16(-float(bq) * m)):
                qaug[kvh, :, g, :, AUG_HI + i] = term
            for i, term in enumerate(_split_bf16(-m)):
                qaug[kvh, :, g, :, AUG_LO + i] = term
            qaug[kvh, :, g, :, AUG_MASK] = NEG_INF
            dbias[kvh, :, g] = -m * np.abs(t[:, None] - t[None, :])
    rows = 2 * GQA_GROUP * bq
    qaug = qaug.reshape(N_KV_HEADS, rows, HEAD_DIM).astype(ml_dtypes.bfloat16)
    dbias = dbias.reshape(N_KV_HEADS, rows, bq).astype(np.float32)
    rel = np.arange(-seq, seq, dtype=np.int64)
    hi = np.floor_divide(rel, bq)
    lo = rel - hi * bq
    sign = np.where(rel >= bq, 1, np.where(rel < 0, -1, 0))
    kaug = np.zeros((2 * seq, HEAD_DIM), np.float64)
    for i in range(N_AUG):
        kaug[:, AUG_SIGN + i] = sign
        kaug[:, AUG_HI + i] = sign * hi
        kaug[:, AUG_LO + i] = sign * lo
    kaug[:, AUG_MASK] = sign == 0
    return jnp.asarray(qaug), jnp.asarray(dbias), jnp.asarray(kaug.astype(ml_dtypes.bfloat16))


def _lane_tiles(x):
    return [x[:, j * HEAD_DIM:(j + 1) * HEAD_DIM] for j in range(x.shape[1] // HEAD_DIM)]


def _diff_attn_kernel(lq1_ref, lk1_ref, lq2_ref, lk2_ref, sg_ref, qaug_ref, dbias_ref, kaug_ref,
                      q_ref, k_ref, v_ref, o_ref,
                      kp_ref, qp_ref, l_ref, ld_ref, mx_ref, den_ref, acc_ref, *, lambda_init):
    bq = DIFF_BQ
    seq = k_ref.shape[0]
    nchunk = seq // DIFF_KC
    n = pl.program_id(2)
    t0 = pl.multiple_of(n * bq, bq)
    dn = (((1,), (1,)), ((), ()))

    @pl.when(n == 0)
    def _():
        kp_ref[:, :HEAD_DIM] = k_ref[...]

    kp_ref[:, HEAD_DIM:] = kaug_ref[pl.ds(pl.multiple_of(seq - t0, bq), seq), :]

    lane = lax.broadcasted_iota(jnp.int32, (bq, HEAD_DIM), 1)
    for c in range(2):
        keep = (lane < DIFF_HEAD_DIM) if c == 0 else (lane >= DIFF_HEAD_DIM)
        for g in range(GQA_GROUP):
            qg = q_ref[:, g * HEAD_DIM:(g + 1) * HEAD_DIM]
            r0 = (c * GQA_GROUP + g) * bq
            qp_ref[r0:r0 + bq, :HEAD_DIM] = jnp.where(keep, qg, jnp.zeros_like(qg))
    qp_ref[:, HEAD_DIM:] = qaug_ref[...]

    ld = lax.dot_general(qp_ref[:, :HEAD_DIM], k_ref[pl.ds(t0, bq), :], dn,
                         preferred_element_type=F32) + dbias_ref[...]
    ld_ref[...] = ld
    mx_ref[...] = ld

    def qk_chunk(c, carry):
        kc = kp_ref[pl.ds(pl.multiple_of(c * DIFF_KC, DIFF_KC), DIFF_KC), :]
        l = lax.dot_general(qp_ref[...], kc, dn, preferred_element_type=F32)
        l_ref[c] = l
        m = mx_ref[...]
        for lt in _lane_tiles(l):
            m = jnp.maximum(m, lt)
        mx_ref[...] = m
        return carry

    lax.fori_loop(0, nchunk, qk_chunk, 0, unroll=DIFF_UNROLL)
    mx_ref[...] = jnp.broadcast_to(jnp.max(mx_ref[...], axis=-1, keepdims=True), mx_ref.shape)

    ed = jnp.exp2(ld_ref[...] - mx_ref[...])
    den_ref[...] = ed
    acc_ref[...] = jnp.dot(ed.astype(BF16), v_ref[pl.ds(t0, bq), :], preferred_element_type=F32)

    def pv_chunk(c, carry):
        mxb = mx_ref[...]
        es = [jnp.exp2(lt - mxb) for lt in _lane_tiles(l_ref[c])]
        den_ref[...] += (es[0] + es[1]) + (es[2] + es[3])
        e = jnp.concatenate([x.astype(BF16) for x in es], axis=-1)
        vc = v_ref[pl.ds(pl.multiple_of(c * DIFF_KC, DIFF_KC), DIFF_KC), :]
        acc_ref[...] += jnp.dot(e, vc, preferred_element_type=F32)
        return carry

    lax.fori_loop(0, nchunk, pv_chunk, 0, unroll=DIFF_UNROLL)

    lam = (jnp.exp(jnp.sum(lq1_ref[...] * lk1_ref[...], axis=-1, keepdims=True))
           - jnp.exp(jnp.sum(lq2_ref[...] * lk2_ref[...], axis=-1, keepdims=True))
           + lambda_init)
    p = acc_ref[...] / jnp.sum(den_ref[...], axis=-1, keepdims=True)
    half_rows = GQA_GROUP * bq
    o = p[:half_rows] - lam * p[half_rows:]
    for g in range(GQA_GROUP):
        og = _rms(o[g * bq:(g + 1) * bq], sg_ref[...]) * (1.0 - lambda_init)
        o_ref[:, g * HEAD_DIM:(g + 1) * HEAD_DIM] = og.astype(o_ref.dtype)


def _diff_attn(qkv, lq1, lk1, lq2, lk2, subln_g, lambda_init):
    bsz, seq, _ = qkv.shape
    assert DIFF_KC == 4 * HEAD_DIM and seq % DIFF_KC == 0
    bq = DIFF_BQ
    rows = 2 * GQA_GROUP * bq
    gw = GQA_GROUP * HEAD_DIM
    qaug, dbias, kaug = _diff_tables(seq)
    vec = lambda w: pl.BlockSpec((1, w), lambda b, h, n: (0, 0))
    per_head = lambda w: pl.BlockSpec((None, rows, w), lambda b, h, n: (h, 0, 0))
    q_spec = pl.BlockSpec((None, bq, gw), lambda b, h, n: (b, n, h))
    k_spec = pl.BlockSpec((None, seq, HEAD_DIM), lambda b, h, n: (b, 0, N_HEADS + h))
    v_spec = pl.BlockSpec((None, seq, HEAD_DIM), lambda b, h, n: (b, 0, N_HEADS + N_KV_HEADS + h))
    return pl.pallas_call(
        functools.partial(_diff_attn_kernel, lambda_init=lambda_init),
        out_shape=jax.ShapeDtypeStruct((bsz, seq, N_HEADS * HEAD_DIM), BF16),
        grid=(bsz, N_KV_HEADS, seq // bq),
        in_specs=[vec(DIFF_HEAD_DIM), vec(DIFF_HEAD_DIM), vec(DIFF_HEAD_DIM), vec(DIFF_HEAD_DIM),
                  vec(HEAD_DIM), per_head(HEAD_DIM), per_head(bq),
                  pl.BlockSpec((2 * seq, HEAD_DIM), lambda b, h, n: (0, 0)),
                  q_spec, k_spec, v_spec],
        out_specs=q_spec,
        scratch_shapes=[
            pltpu.VMEM((seq, 2 * HEAD_DIM), BF16),
            pltpu.VMEM((rows, 2 * HEAD_DIM), BF16),
            pltpu.VMEM((seq // DIFF_KC, rows, DIFF_KC), F32),
            pltpu.VMEM((rows, bq), F32),
            pltpu.VMEM((rows, HEAD_DIM), F32),
            pltpu.VMEM((rows, HEAD_DIM), F32),
            pltpu.VMEM((rows, HEAD_DIM), F32),
        ],
        compiler_params=_params("parallel", "arbitrary", "arbitrary"),
        name="diff_attn",
    )(lq1, lk1, lq2, lk2, subln_g, qaug, dbias, kaug, qkv, qkv, qkv)


def kernel(x, mix_pre_g, mix_post_g, ffn_pre_g, ffn_post_g, w_gate, w_up, w_down, a_w_qkv, a_w_o, a_sink, b_w_qkv, b_w_o, c_w_qkv, c_w_o, c_lambda_q1, c_lambda_k1, c_lambda_q2, c_lambda_k2, c_subln_g):
    bsz, seq, dm = x.shape
    n = bsz * seq
    slopes = jnp.asarray(_alibi_slopes_np(), F32)
    xf = x.reshape(n, dm)
    row = lambda a, i: a[i].reshape(1, -1)
    qkv_stacks, o_stacks = (a_w_qkv, b_w_qkv, c_w_qkv), (a_w_o, b_w_o, c_w_o)
    w_qkv, w_o = a_w_qkv[:1].astype(BF16), a_w_o[:1].astype(BF16)
    ffn_w = tuple(w[:1].astype(BF16) for w in (w_gate, w_up, w_down))
    q_cols = N_HEADS * HEAD_DIM

    for i in range(DEPTH):
        m, j = i % N_MIXERS, i // N_MIXERS
        pre = row(mix_pre_g, i)
        lse = None
        if m == 0:
            qkv = _norm_matmul(xf, pre, w_qkv, 0, q_cols, LOG2E * HEAD_DIM ** -0.5).reshape(bsz, seq, -1)
            o = _window_attn(qkv, slopes, a_sink[j])
        elif m == 1:
            qkv = _norm_matmul(xf, pre, w_qkv, 0, q_cols, LOG2E * HEAD_DIM ** -0.5).reshape(bsz, seq, -1)
            o, lse = _dilated_attn(qkv, slopes)
            lse = lse.reshape(n, dm)
        else:
            lambda_init = 0.8 - 0.6 * math.exp(-0.3 * i)
            qkv = _norm_matmul(xf, pre, w_qkv, 0, q_cols, LOG2E * DIFF_HEAD_DIM ** -0.5).reshape(bsz, seq, -1)
            o = _diff_attn(qkv, row(c_lambda_q1, j), row(c_lambda_k1, j), row(c_lambda_q2, j),
                           row(c_lambda_k2, j), row(c_subln_g, j), lambda_init)
        xf = _out_proj(o.reshape(n, dm), w_o, 0, row(mix_post_g, i), xf, lse=lse)
        if i + 1 < DEPTH:
            m1, j1 = (i + 1) % N_MIXERS, (i + 1) // N_MIXERS
            xf, ffn_w, (w_qkv, w_o) = _ffn(xf, row(ffn_pre_g, i), *ffn_w, row(ffn_post_g, i),
                                           next_ffn=(w_gate, w_up, w_down, i + 1),
                                           slab_casts=((qkv_stacks[m1], j1), (o_stacks[m1], j1)))
        else:
            xf, _, _ = _ffn(xf, row(ffn_pre_g, i), *ffn_w, row(ffn_post_g, i))
    return xf.reshape(bsz, seq, dm)
```

```python
import functools
import math

import jax
import jax.numpy as jnp
import ml_dtypes
import numpy as np
from jax import lax
from jax.experimental import pallas as pl
from jax.experimental.pallas import tpu as pltpu

F32 = jnp.float32
BF16 = jnp.bfloat16

D_MODEL = 2048
DEPTH = 4
N_MIXERS = 3
HEAD_DIM = 128
N_HEADS = 16
N_KV_HEADS = 4
GQA_GROUP = N_HEADS // N_KV_HEADS
DIFF_HEAD_DIM = HEAD_DIM // 2
WINDOW_A = 128
DILATED_PATTERNS = ((128, 1), (512, 4), (2048, 16))
B_GROUP_HEADS = (6, 5, 5)
FFN_HIDDEN = 5632
RMS_EPS = 1e-6
NEG_INF = -1e30
LOG2E = math.log2(math.e)

VMEM_LIMIT_BYTES = 48 * 1024 * 1024
FFN_VMEM_LIMIT_BYTES = 56 * 1024 * 1024

PROJ_TM, PROJ_TM_RESIDUAL, PROJ_TN = 1024, 512, 1024
OUT_TM = 512
FFN_TM, FFN_TH = 1024, 512
SLAB_SPLIT = 4
LOCAL_BQ = 128
LOCAL_GROUP = 8
DIFF_BQ = 128
DIFF_KC = 512
DIFF_UNROLL = 4
N_AUG = 3


def _alibi_slopes_np():
    return 2.0 ** (-8.0 * np.arange(1, N_HEADS + 1, dtype=np.float64) / N_HEADS)


def _rms(x, g):
    return x * lax.rsqrt(jnp.mean(x * x, axis=-1, keepdims=True) + RMS_EPS) * g


def _params(*sem, vmem_limit_bytes=VMEM_LIMIT_BYTES):
    return pltpu.CompilerParams(dimension_semantics=sem, vmem_limit_bytes=vmem_limit_bytes)


def _norm_matmul_kernel(*refs, scaled_tiles, scale, has_delta):
    if has_delta:
        x_ref, d_ref, g_ref, w_ref, o_ref, xs_ref, h_ref = refs
    else:
        x_ref, g_ref, w_ref, o_ref, h_ref = refs
    j = pl.program_id(1)

    @pl.when(j == 0)
    def _():
        x = x_ref[...]
        if has_delta:
            x = x + d_ref[...]
            xs_ref[...] = x
        h_ref[...] = _rms(x, g_ref[...]).astype(BF16)

    acc = jnp.dot(h_ref[...], w_ref[...], preferred_element_type=F32)
    if scaled_tiles:
        acc = acc * jnp.where(j < scaled_tiles, scale, 1.0)
    o_ref[...] = acc.astype(o_ref.dtype)


def _norm_matmul(x, delta, g, w, scaled_cols=0, scale=1.0):
    n, dm = x.shape
    m = w.shape[2]
    assert scaled_cols % PROJ_TN == 0
    has_delta = delta is not None
    tm = PROJ_TM_RESIDUAL if has_delta else PROJ_TM
    x_spec = pl.BlockSpec((tm, dm), lambda i, j: (i, 0))
    o_spec = pl.BlockSpec((tm, PROJ_TN), lambda i, j: (i, j))
    res = pl.pallas_call(
        functools.partial(_norm_matmul_kernel, scaled_tiles=scaled_cols // PROJ_TN, scale=scale,
                          has_delta=has_delta),
        out_shape=[jax.ShapeDtypeStruct((n, m), BF16)] + [jax.ShapeDtypeStruct((n, dm), F32)] * has_delta,
        grid=(n // tm, m // PROJ_TN),
        in_specs=[x_spec] * (1 + has_delta) + [
            pl.BlockSpec((1, dm), lambda i, j: (0, 0)),
            pl.BlockSpec((None, dm, PROJ_TN), lambda i, j: (0, 0, j)),
        ],
        out_specs=[o_spec] + [x_spec] * has_delta,
        scratch_shapes=[pltpu.VMEM((tm, dm), BF16)],
        compiler_params=_params("parallel", "arbitrary"),
        name="norm_matmul",
    )(*([x] + [delta] * has_delta + [g, w]))
    return res[0], (res[1] if has_delta else x)


def _out_tail(y, g_ref, x_ref, gn_ref, out_ref, hn_ref):
    x1 = x_ref[...] + _rms(y, g_ref[...])
    out_ref[...] = x1
    hn_ref[...] = _rms(x1, gn_ref[...]).astype(BF16)


def _out_kernel(o_ref, w_ref, g_ref, x_ref, gn_ref, out_ref, hn_ref):
    y = jnp.dot(o_ref[...], w_ref[...], preferred_element_type=F32)
    _out_tail(y, g_ref, x_ref, gn_ref, out_ref, hn_ref)


def _out_merge_kernel(o_ref, lse_ref, w_ref, g_ref, x_ref, gn_ref, out_ref, hn_ref):
    scores = []
    h0 = 0
    for hg in B_GROUP_HEADS:
        acc = lse_ref[:, h0 * HEAD_DIM:(h0 + 1) * HEAD_DIM]
        for h in range(h0 + 1, h0 + hg):
            acc = acc + lse_ref[:, h * HEAD_DIM:(h + 1) * HEAD_DIM]
        scores.append(acc * (1.0 / hg))
        h0 += hg
    mx = jnp.maximum(jnp.maximum(scores[0], scores[1]), scores[2])
    es = [jnp.exp(s - mx) for s in scores]
    inv = 1.0 / (es[0] + es[1] + es[2])
    pieces = []
    h0 = 0
    for gi, hg in enumerate(B_GROUP_HEADS):
        alpha = es[gi] * inv
        for h in range(h0, h0 + hg):
            oh = o_ref[:, h * HEAD_DIM:(h + 1) * HEAD_DIM].astype(F32)
            pieces.append((oh * alpha).astype(BF16))
        h0 += hg
    mixed = jnp.concatenate(pieces, axis=-1)
    y = jnp.dot(mixed, w_ref[...], preferred_element_type=F32)
    _out_tail(y, g_ref, x_ref, gn_ref, out_ref, hn_ref)


def _out_proj(o, w, g, x, g_next, lse=None):
    n, dm = x.shape
    row = lambda i: (i, 0)
    fixed = lambda i: (0, 0)
    o_spec = pl.BlockSpec((OUT_TM, dm), row)
    w_spec = pl.BlockSpec((None, dm, dm), lambda i: (0, 0, 0))
    g_spec = pl.BlockSpec((1, dm), fixed)
    if lse is None:
        body, args = _out_kernel, (o, w, g, x, g_next)
        in_specs = [o_spec, w_spec, g_spec, o_spec, g_spec]
    else:
        body, args = _out_merge_kernel, (o, lse, w, g, x, g_next)
        in_specs = [o_spec, o_spec, w_spec, g_spec, o_spec, g_spec]
    return pl.pallas_call(
        body,
        out_shape=[jax.ShapeDtypeStruct((n, dm), F32), jax.ShapeDtypeStruct((n, dm), BF16)],
        grid=(n // OUT_TM,),
        in_specs=in_specs,
        out_specs=[o_spec, o_spec],
        compiler_params=_params("parallel"),
        name="out_proj",
    )(*args)


def _ffn_kernel(h_ref, wg_ref, wu_ref, wd_ref, gpost_ref, *rest, n_tile_casts, n_slab_casts):
    n_casts = n_tile_casts + n_slab_casts
    cast_in, out_ref, cast_out = rest[:n_casts], rest[n_casts], rest[n_casts + 1:]
    k = pl.program_id(1)

    for src, dst in zip(cast_in[:n_tile_casts], cast_out[:n_tile_casts]):
        dst[...] = src[...].astype(BF16)

    @pl.when(k == 0)
    def _():
        out_ref[...] = jnp.zeros_like(out_ref)

    @pl.when(k < SLAB_SPLIT)
    def _():
        for src, dst in zip(cast_in[n_tile_casts:], cast_out[n_tile_casts:]):
            dst[...] = src[...].astype(BF16)

    h = h_ref[...]
    gate = jnp.dot(h, wg_ref[...], preferred_element_type=F32)
    up = jnp.dot(h, wu_ref[...], preferred_element_type=F32)
    act = (gate * jax.nn.sigmoid(gate) * up).astype(BF16)
    out_ref[...] += jnp.dot(act, wd_ref[...], preferred_element_type=F32)

    @pl.when(k == pl.num_programs(1) - 1)
    def _():
        out_ref[...] = _rms(out_ref[...], gpost_ref[...])


def _ffn(h, wg, wu, wd, gpost, next_ffn=None, slab_casts=()):
    n, dm = h.shape
    hid = wg.shape[2]
    grid = (n // FFN_TM, hid // FFN_TH)
    assert grid[1] >= SLAB_SPLIT
    row = lambda i, k: (i, 0)
    fixed = lambda i, k: (0, 0)
    in_specs = [
        pl.BlockSpec((FFN_TM, dm), row),
        pl.BlockSpec((None, dm, FFN_TH), lambda i, k: (0, 0, k)),
        pl.BlockSpec((None, dm, FFN_TH), lambda i, k: (0, 0, k)),
        pl.BlockSpec((None, FFN_TH, dm), lambda i, k: (0, k, 0)),
        pl.BlockSpec((1, dm), fixed),
    ]
    args = [h, wg, wu, wd, gpost]
    out_shape = [jax.ShapeDtypeStruct((n, dm), F32)]
    out_specs = [pl.BlockSpec((FFN_TM, dm), row)]

    def add_cast(stack, block, in_map, out_map):
        in_specs.append(pl.BlockSpec((None,) + block, in_map))
        args.append(stack)
        out_shape.append(jax.ShapeDtypeStruct((1,) + stack.shape[1:], BF16))
        out_specs.append(pl.BlockSpec((None,) + block, out_map))

    n_tile_casts = 0
    if next_ffn is not None:
        ng, nu, nd, layer = next_ffn
        cr = dm // grid[0]
        assert dm % grid[0] == 0 and cr % 8 == 0
        add_cast(ng, (cr, FFN_TH), lambda i, k: (layer, i, k), lambda i, k: (0, i, k))
        add_cast(nu, (cr, FFN_TH), lambda i, k: (layer, i, k), lambda i, k: (0, i, k))
        add_cast(nd, (FFN_TH, cr), lambda i, k: (layer, k, i), lambda i, k: (0, k, i))
        n_tile_casts = 3
    for stack, index in slab_casts:
        rows = stack.shape[1] // (grid[0] * SLAB_SPLIT)
        assert stack.shape[1] % (grid[0] * SLAB_SPLIT) == 0 and rows % 8 == 0
        slab = lambda i, k: SLAB_SPLIT * i + jnp.minimum(k, SLAB_SPLIT - 1)
        add_cast(stack, (rows, stack.shape[2]),
                 lambda i, k, index=index: (index, slab(i, k), 0), lambda i, k: (0, slab(i, k), 0))
    res = pl.pallas_call(
        functools.partial(_ffn_kernel, n_tile_casts=n_tile_casts, n_slab_casts=len(slab_casts)),
        out_shape=out_shape,
        grid=grid,
        in_specs=in_specs,
        out_specs=out_specs,
        compiler_params=_params("parallel", "arbitrary", vmem_limit_bytes=FFN_VMEM_LIMIT_BYTES),
        name="ffn",
    )(*args)
    return res[0], tuple(res[1:1 + n_tile_casts]), tuple(res[1 + n_tile_casts:])


def _add_kernel(a_ref, b_ref, o_ref):
    o_ref[...] = a_ref[...] + b_ref[...]


def _add(a, b):
    n, dm = a.shape
    spec = pl.BlockSpec((FFN_TM, dm), lambda i: (i, 0))
    return pl.pallas_call(
        _add_kernel,
        out_shape=jax.ShapeDtypeStruct((n, dm), F32),
        grid=(n // FFN_TM,),
        in_specs=[spec, spec],
        out_specs=spec,
        compiler_params=_params("parallel"),
        name="residual_add",
    )(a, b)


def _local_attn_body(slope, sink, q_ref, k_ref, v_ref, o_ref, lse_ref, bm_ref, scratch, *, half, dil):
    seq = q_ref.shape[0]
    bq = LOCAL_BQ
    lc = seq // dil
    ks = min(3 * bq, lc)
    nqb = lc // bq
    if dil > 1:
        qf, kf, vf, of = scratch
        qf[...] = q_ref[...].astype(F32)
        kf[...] = k_ref[...].astype(F32)
        vf[...] = v_ref[...].astype(F32)
    rel0 = (lax.broadcasted_iota(jnp.int32, (bq, ks), 0)
            - lax.broadcasted_iota(jnp.int32, (bq, ks), 1))
    step_slope = slope * (dil * LOG2E)
    for case in range(3):
        dist = jnp.abs(rel0 + case * bq)
        bm_ref[case, :, :ks] = jnp.where(dist <= half, -step_slope * dist.astype(F32), NEG_INF)
    sink2 = None if sink is None else sink * LOG2E

    def logits(idx):
        r = idx // nqb
        q0 = (idx % nqb) * bq
        k0 = jnp.clip(q0 - bq, 0, lc - ks)
        if dil == 1:
            rows = pl.ds(pl.multiple_of(q0, bq), bq)
            q = q_ref[rows, :]
            k = k_ref[pl.ds(pl.multiple_of(k0, bq), ks), :]
            v = v_ref[pl.ds(pl.multiple_of(k0, bq), ks), :]
        else:
            rows = pl.ds(r + dil * q0, bq, stride=dil)
            q = qf[rows, :].astype(BF16)
            k = kf[pl.ds(r + dil * k0, ks, stride=dil), :].astype(BF16)
            v = vf[pl.ds(r + dil * k0, ks, stride=dil), :].astype(BF16)
        s = lax.dot_general(q, k, (((1,), (1,)), ((), ())), preferred_element_type=F32)
        return s + bm_ref[(q0 - k0) // bq, :, :ks], v, rows

    def softmax(s):
        mx = jnp.max(s, axis=-1, keepdims=True)
        if sink2 is not None:
            mx = jnp.maximum(mx, sink2)
        e = jnp.exp2(s - mx)
        den = jnp.sum(e, axis=-1, keepdims=True)
        if sink2 is not None:
            den = den + jnp.exp2(sink2 - mx)
        return e.astype(BF16), mx, den

    def finish(e, mx, den, v, rows):
        o = jnp.dot(e, v, preferred_element_type=F32) / den
        if dil == 1:
            o_ref[rows, :] = o.astype(o_ref.dtype)
        else:
            of[rows, :] = o
        if lse_ref is not None:
            lse = mx * (1.0 / LOG2E) + jnp.log(den)
            lse_ref[rows, :] = jnp.broadcast_to(lse, (bq, HEAD_DIM))

    def group(gidx, carry):
        staged = [logits(gidx * LOCAL_GROUP + u) for u in range(LOCAL_GROUP)]
        probs = [softmax(s) for s, _, _ in staged]
        for (e, mx, den), (_, v, rows) in zip(probs, staged):
            finish(e, mx, den, v, rows)
        return carry

    lax.fori_loop(0, dil * nqb // LOCAL_GROUP, group, 0)
    if dil > 1:
        o_ref[...] = of[...].astype(o_ref.dtype)


def _window_attn_kernel(slope_ref, sink_ref, q_ref, k_ref, v_ref, o_ref, bm_ref):
    head = pl.program_id(1)
    _local_attn_body(slope_ref[head], sink_ref[head], q_ref, k_ref, v_ref, o_ref, None, bm_ref, None,
                     half=WINDOW_A, dil=1)


def _dilated_attn_kernel(slope_ref, q_ref, k_ref, v_ref, o_ref, lse_ref, bm_ref, *scratch):
    head = pl.program_id(1)
    h0 = 0
    for (window, dil), hg in zip(DILATED_PATTERNS, B_GROUP_HEADS):
        @pl.when((head >= h0) & (head < h0 + hg))
        def _(window=window, dil=dil):
            _local_attn_body(slope_ref[head], None, q_ref, k_ref, v_ref, o_ref, lse_ref, bm_ref, scratch,
                             half=window // (2 * dil), dil=dil)
        h0 += hg


_BIAS_TILES = pltpu.VMEM((3, LOCAL_BQ, 3 * LOCAL_BQ), F32)


def _head_spec(seq, col0, group=1):
    return pl.BlockSpec((None, seq, HEAD_DIM), lambda b, h: (b, 0, col0 + h // group))


def _window_attn(qkv, slopes, sink):
    bsz, seq, _ = qkv.shape
    smem = pl.BlockSpec(memory_space=pltpu.SMEM)
    return pl.pallas_call(
        _window_attn_kernel,
        out_shape=jax.ShapeDtypeStruct((bsz, seq, N_HEADS * HEAD_DIM), BF16),
        grid=(bsz, N_HEADS),
        in_specs=[smem, smem, _head_spec(seq, 0), _head_spec(seq, N_HEADS, GQA_GROUP),
                  _head_spec(seq, N_HEADS + N_KV_HEADS, GQA_GROUP)],
        out_specs=_head_spec(seq, 0),
        scratch_shapes=[_BIAS_TILES],
        compiler_params=_params("parallel", "arbitrary"),
        name="window_attn",
    )(slopes, sink, qkv, qkv, qkv)


def _dilated_attn(qkv, slopes):
    bsz, seq, _ = qkv.shape
    width = N_HEADS * HEAD_DIM
    return pl.pallas_call(
        _dilated_attn_kernel,
        out_shape=[jax.ShapeDtypeStruct((bsz, seq, width), BF16),
                   jax.ShapeDtypeStruct((bsz, seq, width), F32)],
        grid=(bsz, N_HEADS),
        in_specs=[pl.BlockSpec(memory_space=pltpu.SMEM), _head_spec(seq, 0),
                  _head_spec(seq, N_HEADS), _head_spec(seq, 2 * N_HEADS)],
        out_specs=[_head_spec(seq, 0), _head_spec(seq, 0)],
        scratch_shapes=[_BIAS_TILES] + [pltpu.VMEM((seq, HEAD_DIM), F32) for _ in range(4)],
        compiler_params=_params("parallel", "arbitrary"),
        name="dilated_attn",
    )(slopes, qkv, qkv, qkv)


AUG_SIGN, AUG_HI, AUG_LO = 0, N_AUG, 2 * N_AUG


def _split_bf16(x, n=N_AUG):
    terms, rest = [], np.asarray(x, np.float64)
    for _ in range(n):
        t = rest.astype(ml_dtypes.bfloat16).astype(np.float64)
        terms.append(t)
        rest = rest - t
    return terms


def _diff_tables(seq):
    bq = DIFF_BQ
    m2 = _alibi_slopes_np() * LOG2E
    t = np.arange(bq, dtype=np.float64)
    qaug = np.zeros((N_KV_HEADS, 2, GQA_GROUP, bq, HEAD_DIM), np.float64)
    dbias = np.zeros((N_KV_HEADS, 2, GQA_GROUP, bq, bq), np.float64)
    for kvh in range(N_KV_HEADS):
        for g in range(GQA_GROUP):
            m = m2[kvh * GQA_GROUP + g]
            for i, term in enumerate(_split_bf16(m * t)):
                qaug[kvh, :, g, :, AUG_SIGN + i] = term
            for i, term in enumerate(_split_bf16(-float(bq) * m)):
                qaug[kvh, :, g, :, AUG_HI + i] = term
            for i, term in enumerate(_split_bf16(-m)):
                qaug[kvh, :, g, :, AUG_LO + i] = term
            dbias[kvh, :, g] = -m * np.abs(t[:, None] - t[None, :])
    rows = 2 * GQA_GROUP * bq
    qaug = qaug.reshape(N_KV_HEADS, rows, HEAD_DIM).astype(ml_dtypes.bfloat16)
    dbias = dbias.reshape(N_KV_HEADS, rows, bq).astype(np.float32)
    rel = np.arange(-seq, seq, dtype=np.int64)
    hi = np.floor_divide(rel, bq)
    lo = rel - hi * bq
    sign = np.where(rel >= bq, 1, np.where(rel < 0, -1, 0))
    kaug = np.zeros((2 * seq, HEAD_DIM), np.float64)
    for i in range(N_AUG):
        kaug[:, AUG_SIGN + i] = sign
        kaug[:, AUG_HI + i] = sign * hi
        kaug[:, AUG_LO + i] = sign * lo
    return jnp.asarray(qaug), jnp.asarray(dbias), jnp.asarray(kaug.astype(ml_dtypes.bfloat16))


def _lane_tiles(x):
    return [x[:, j * HEAD_DIM:(j + 1) * HEAD_DIM] for j in range(x.shape[1] // HEAD_DIM)]


def _diff_attn_kernel(lq1_ref, lk1_ref, lq2_ref, lk2_ref, sg_ref, qaug_ref, dbias_ref, kaug_ref,
                      q_ref, k_ref, v_ref, o_ref,
                      kp_ref, qp_ref, l_ref, mx_ref, den_ref, acc_ref, *, lambda_init):
    bq = DIFF_BQ
    seq = k_ref.shape[0]
    nchunk = seq // DIFF_KC
    n = pl.program_id(2)
    t0 = n * bq
    dn = (((1,), (1,)), ((), ()))

    @pl.when(n == 0)
    def _():
        kp_ref[:, :HEAD_DIM] = k_ref[...]

    kp_ref[:, HEAD_DIM:] = kaug_ref[pl.ds(pl.multiple_of(seq - t0, bq), seq), :]

    lane = lax.broadcasted_iota(jnp.int32, (bq, HEAD_DIM), 1)
    for c in range(2):
        keep = (lane < DIFF_HEAD_DIM) if c == 0 else (lane >= DIFF_HEAD_DIM)
        for g in range(GQA_GROUP):
            qg = q_ref[:, g * HEAD_DIM:(g + 1) * HEAD_DIM]
            r0 = (c * GQA_GROUP + g) * bq
            qp_ref[r0:r0 + bq, :HEAD_DIM] = jnp.where(keep, qg, jnp.zeros_like(qg))
    qp_ref[:, HEAD_DIM:] = qaug_ref[...]

    tiles_per_chunk = DIFF_KC // HEAD_DIM
    for c in range(nchunk):
        l = lax.dot_general(qp_ref[...], kp_ref[c * DIFF_KC:(c + 1) * DIFF_KC, :], dn,
                            preferred_element_type=F32)
        tiles = []
        for j, lt in enumerate(_lane_tiles(l)):
            own = (c * tiles_per_chunk + j) == n
            tiles.append(lt + jnp.where(own, dbias_ref[...], 0.0))
        l_ref[c] = jnp.concatenate(tiles, axis=-1)
        m = jnp.maximum(jnp.maximum(tiles[0], tiles[1]), jnp.maximum(tiles[2], tiles[3]))
        mx_ref[...] = m if c == 0 else jnp.maximum(mx_ref[...], m)
    mx_ref[...] = jnp.broadcast_to(jnp.max(mx_ref[...], axis=-1, keepdims=True), mx_ref.shape)
    den_ref[...] = jnp.zeros_like(den_ref)
    acc_ref[...] = jnp.zeros_like(acc_ref)

    def pv_chunk(c, carry):
        mxb = mx_ref[...]
        es = [jnp.exp2(lt - mxb) for lt in _lane_tiles(l_ref[c])]
        den_ref[...] += (es[0] + es[1]) + (es[2] + es[3])
        e = jnp.concatenate([x.astype(BF16) for x in es], axis=-1)
        vc = v_ref[pl.ds(pl.multiple_of(c * DIFF_KC, DIFF_KC), DIFF_KC), :]
        acc_ref[...] += jnp.dot(e, vc, preferred_element_type=F32)
        return carry

    lax.fori_loop(0, nchunk, pv_chunk, 0, unroll=DIFF_UNROLL)

    lam = (jnp.exp(jnp.sum(lq1_ref[...] * lk1_ref[...], axis=-1, keepdims=True))
           - jnp.exp(jnp.sum(lq2_ref[...] * lk2_ref[...], axis=-1, keepdims=True))
           + lambda_init)
    p = acc_ref[...] / jnp.sum(den_ref[...], axis=-1, keepdims=True)
    half_rows = GQA_GROUP * bq
    o = p[:half_rows] - lam * p[half_rows:]
    for g in range(GQA_GROUP):
        og = _rms(o[g * bq:(g + 1) * bq], sg_ref[...]) * (1.0 - lambda_init)
        o_ref[:, g * HEAD_DIM:(g + 1) * HEAD_DIM] = og.astype(o_ref.dtype)


def _diff_attn(qkv, lq1, lk1, lq2, lk2, subln_g, lambda_init):
    bsz, seq, _ = qkv.shape
    assert DIFF_KC == 4 * HEAD_DIM and seq % DIFF_KC == 0
    bq = DIFF_BQ
    rows = 2 * GQA_GROUP * bq
    gw = GQA_GROUP * HEAD_DIM
    qaug, dbias, kaug = _diff_tables(seq)
    vec = lambda w: pl.BlockSpec((1, w), lambda b, h, n: (0, 0))
    per_head = lambda w: pl.BlockSpec((None, rows, w), lambda b, h, n: (h, 0, 0))
    q_spec = pl.BlockSpec((None, bq, gw), lambda b, h, n: (b, n, h))
    k_spec = pl.BlockSpec((None, seq, HEAD_DIM), lambda b, h, n: (b, 0, N_HEADS + h))
    v_spec = pl.BlockSpec((None, seq, HEAD_DIM), lambda b, h, n: (b, 0, N_HEADS + N_KV_HEADS + h))
    return pl.pallas_call(
        functools.partial(_diff_attn_kernel, lambda_init=lambda_init),
        out_shape=jax.ShapeDtypeStruct((bsz, seq, N_HEADS * HEAD_DIM), BF16),
        grid=(bsz, N_KV_HEADS, seq // bq),
        in_specs=[vec(DIFF_HEAD_DIM), vec(DIFF_HEAD_DIM), vec(DIFF_HEAD_DIM), vec(DIFF_HEAD_DIM),
                  vec(HEAD_DIM), per_head(HEAD_DIM), per_head(bq),
                  pl.BlockSpec((2 * seq, HEAD_DIM), lambda b, h, n: (0, 0)),
                  q_spec, k_spec, v_spec],
        out_specs=q_spec,
        scratch_shapes=[
            pltpu.VMEM((seq, 2 * HEAD_DIM), BF16),
            pltpu.VMEM((rows, 2 * HEAD_DIM), BF16),
            pltpu.VMEM((seq // DIFF_KC, rows, DIFF_KC), F32),
            pltpu.VMEM((rows, HEAD_DIM), F32),
            pltpu.VMEM((rows, HEAD_DIM), F32),
            pltpu.VMEM((rows, HEAD_DIM), F32),
        ],
        compiler_params=_params("parallel", "arbitrary", "arbitrary"),
        name="diff_attn",
    )(lq1, lk1, lq2, lk2, subln_g, qaug, dbias, kaug, qkv, qkv, qkv)


def kernel(x, mix_pre_g, mix_post_g, ffn_pre_g, ffn_post_g, w_gate, w_up, w_down, a_w_qkv, a_w_o, a_sink, b_w_qkv, b_w_o, c_w_qkv, c_w_o, c_lambda_q1, c_lambda_k1, c_lambda_q2, c_lambda_k2, c_subln_g):
    bsz, seq, dm = x.shape
    n = bsz * seq
    slopes = jnp.asarray(_alibi_slopes_np(), F32)
    xf = x.reshape(n, dm)
    row = lambda a, i: a[i].reshape(1, -1)
    qkv_stacks, o_stacks = (a_w_qkv, b_w_qkv, c_w_qkv), (a_w_o, b_w_o, c_w_o)
    w_qkv, w_o = a_w_qkv[:1].astype(BF16), a_w_o[:1].astype(BF16)
    ffn_w = tuple(w[:1].astype(BF16) for w in (w_gate, w_up, w_down))
    q_cols = N_HEADS * HEAD_DIM
    delta = None

    for i in range(DEPTH):
        m, j = i % N_MIXERS, i // N_MIXERS
        pre = row(mix_pre_g, i)
        head_dim = DIFF_HEAD_DIM if m == 2 else HEAD_DIM
        qkv, xf = _norm_matmul(xf, delta, pre, w_qkv, q_cols, LOG2E * head_dim ** -0.5)
        qkv = qkv.reshape(bsz, seq, -1)
        lse = None
        if m == 0:
            o = _window_attn(qkv, slopes, a_sink[j])
        elif m == 1:
            o, lse = _dilated_attn(qkv, slopes)
            lse = lse.reshape(n, dm)
        else:
            lambda_init = 0.8 - 0.6 * math.exp(-0.3 * i)
            o = _diff_attn(qkv, row(c_lambda_q1, j), row(c_lambda_k1, j), row(c_lambda_q2, j),
                           row(c_lambda_k2, j), row(c_subln_g, j), lambda_init)
        xf, h = _out_proj(o.reshape(n, dm), w_o, row(mix_post_g, i), xf, row(ffn_pre_g, i), lse=lse)
        if i + 1 < DEPTH:
            m1, j1 = (i + 1) % N_MIXERS, (i + 1) // N_MIXERS
            delta, ffn_w, (w_qkv, w_o) = _ffn(h, *ffn_w, row(ffn_post_g, i),
                                              next_ffn=(w_gate, w_up, w_down, i + 1),
                                              slab_casts=((qkv_stacks[m1], j1), (o_stacks[m1], j1)))
        else:
            delta, _, _ = _ffn(h, *ffn_w, row(ffn_post_g, i))
    xf = _add(xf, delta)
    return xf.reshape(bsz, seq, dm)
```

```python
import functools
import math

import jax
import jax.numpy as jnp
import ml_dtypes
import numpy as np
from jax import lax
from jax.experimental import pallas as pl
from jax.experimental.pallas import tpu as pltpu

F32 = jnp.float32
BF16 = jnp.bfloat16

D_MODEL = 2048
DEPTH = 4
N_MIXERS = 3
HEAD_DIM = 128
N_HEADS = 16
N_KV_HEADS = 4
GQA_GROUP = N_HEADS // N_KV_HEADS
DIFF_HEAD_DIM = HEAD_DIM // 2
WINDOW_A = 128
DILATED_PATTERNS = ((128, 1), (512, 4), (2048, 16))
B_GROUP_HEADS = (6, 5, 5)
FFN_HIDDEN = 5632
RMS_EPS = 1e-6
NEG_INF = -1e30
LOG2E = math.log2(math.e)

VMEM_LIMIT_BYTES = 48 * 1024 * 1024
WIDE_VMEM_LIMIT_BYTES = 60 * 1024 * 1024
FFN_VMEM_LIMIT_BYTES = WIDE_VMEM_LIMIT_BYTES

PROJ_TM, PROJ_TN, PROJ_TN_RESIDUAL = 1024, 1024, 512
NORM_ROWS = 256
OUT_TM = 512
FFN_TM, FFN_TH = 1024, 512
SLAB_SPLIT = 4
LOCAL_BQ = 128
LOCAL_GROUP = 8
DIFF_BQ = 128
DIFF_KC = 512
DIFF_UNROLL = 4
N_AUG = 3


def _alibi_slopes_np():
    return 2.0 ** (-8.0 * np.arange(1, N_HEADS + 1, dtype=np.float64) / N_HEADS)


def _rms(x, g):
    return x * lax.rsqrt(jnp.mean(x * x, axis=-1, keepdims=True) + RMS_EPS) * g


def _params(*sem, vmem_limit_bytes=VMEM_LIMIT_BYTES):
    return pltpu.CompilerParams(dimension_semantics=sem, vmem_limit_bytes=vmem_limit_bytes)


def _norm_matmul_kernel(*refs, scaled_tiles, scale, has_delta):
    if has_delta:
        x_ref, d_ref, g_ref, w_ref, o_ref, h_ref = refs
    else:
        x_ref, g_ref, w_ref, o_ref, h_ref = refs
    j = pl.program_id(1)

    @pl.when(j == 0)
    def _():
        for r0 in range(0, x_ref.shape[0], NORM_ROWS):
            rows = slice(r0, r0 + NORM_ROWS)
            x = x_ref[rows, :]
            if has_delta:
                x = x + d_ref[rows, :]
            h_ref[rows, :] = _rms(x, g_ref[...]).astype(BF16)

    acc = jnp.dot(h_ref[...], w_ref[...], preferred_element_type=F32)
    if scaled_tiles:
        acc = acc * jnp.where(j < scaled_tiles, scale, 1.0)
    o_ref[...] = acc.astype(o_ref.dtype)


def _norm_matmul(x, delta, g, w, scaled_cols=0, scale=1.0):
    n, dm = x.shape
    m = w.shape[2]
    has_delta = delta is not None
    tn = PROJ_TN_RESIDUAL if has_delta else PROJ_TN
    assert scaled_cols % tn == 0
    x_spec = pl.BlockSpec((PROJ_TM, dm), lambda i, j: (i, 0))
    return pl.pallas_call(
        functools.partial(_norm_matmul_kernel, scaled_tiles=scaled_cols // tn, scale=scale,
                          has_delta=has_delta),
        out_shape=jax.ShapeDtypeStruct((n, m), BF16),
        grid=(n // PROJ_TM, m // tn),
        in_specs=[x_spec] * (1 + has_delta) + [
            pl.BlockSpec((1, dm), lambda i, j: (0, 0)),
            pl.BlockSpec((None, dm, tn), lambda i, j: (0, 0, j)),
        ],
        out_specs=pl.BlockSpec((PROJ_TM, tn), lambda i, j: (i, j)),
        scratch_shapes=[pltpu.VMEM((PROJ_TM, dm), BF16)],
        compiler_params=_params("parallel", "arbitrary", vmem_limit_bytes=WIDE_VMEM_LIMIT_BYTES),
        name="norm_matmul",
    )(*([x] + [delta] * has_delta + [g, w]))


def _out_tail(y, g_ref, gn_ref, x_ref, rest):
    out_ref, hn_ref = rest[-2:]
    x = x_ref[...]
    if len(rest) == 3:
        x = x + rest[0][...]
    x1 = x + _rms(y, g_ref[...])
    out_ref[...] = x1
    hn_ref[...] = _rms(x1, gn_ref[...]).astype(BF16)


def _out_kernel(o_ref, w_ref, g_ref, gn_ref, x_ref, *rest):
    y = jnp.dot(o_ref[...], w_ref[...], preferred_element_type=F32)
    _out_tail(y, g_ref, gn_ref, x_ref, rest)


def _out_merge_kernel(o_ref, lse_ref, w_ref, g_ref, gn_ref, x_ref, *rest):
    scores = []
    h0 = 0
    for hg in B_GROUP_HEADS:
        acc = lse_ref[:, h0 * HEAD_DIM:(h0 + 1) * HEAD_DIM]
        for h in range(h0 + 1, h0 + hg):
            acc = acc + lse_ref[:, h * HEAD_DIM:(h + 1) * HEAD_DIM]
        scores.append(acc * (1.0 / hg))
        h0 += hg
    mx = jnp.maximum(jnp.maximum(scores[0], scores[1]), scores[2])
    es = [jnp.exp(s - mx) for s in scores]
    inv = 1.0 / (es[0] + es[1] + es[2])
    pieces = []
    h0 = 0
    for gi, hg in enumerate(B_GROUP_HEADS):
        alpha = es[gi] * inv
        for h in range(h0, h0 + hg):
            oh = o_ref[:, h * HEAD_DIM:(h + 1) * HEAD_DIM].astype(F32)
            pieces.append((oh * alpha).astype(BF16))
        h0 += hg
    mixed = jnp.concatenate(pieces, axis=-1)
    y = jnp.dot(mixed, w_ref[...], preferred_element_type=F32)
    _out_tail(y, g_ref, gn_ref, x_ref, rest)


def _out_proj(o, w, g, g_next, x, delta=None, lse=None):
    n, dm = x.shape
    row = lambda i: (i, 0)
    fixed = lambda i: (0, 0)
    o_spec = pl.BlockSpec((OUT_TM, dm), row)
    w_spec = pl.BlockSpec((None, dm, dm), lambda i: (0, 0, 0))
    g_spec = pl.BlockSpec((1, dm), fixed)
    if lse is None:
        body, args, in_specs = _out_kernel, [o], [o_spec]
    else:
        body, args, in_specs = _out_merge_kernel, [o, lse], [o_spec, o_spec]
    args += [w, g, g_next, x] + ([] if delta is None else [delta])
    in_specs += [w_spec, g_spec, g_spec, o_spec] + ([] if delta is None else [o_spec])
    return pl.pallas_call(
        body,
        out_shape=[jax.ShapeDtypeStruct((n, dm), F32), jax.ShapeDtypeStruct((n, dm), BF16)],
        grid=(n // OUT_TM,),
        in_specs=in_specs,
        out_specs=[o_spec, o_spec],
        compiler_params=_params("parallel", vmem_limit_bytes=WIDE_VMEM_LIMIT_BYTES),
        name="out_proj",
    )(*args)


def _ffn_kernel(h_ref, wg_ref, wu_ref, wd_ref, gpost_ref, *rest, n_tile_casts, n_slab_casts):
    n_casts = n_tile_casts + n_slab_casts
    cast_in, out_ref, cast_out = rest[:n_casts], rest[n_casts], rest[n_casts + 1:]
    k = pl.program_id(1)

    for src, dst in zip(cast_in[:n_tile_casts], cast_out[:n_tile_casts]):
        dst[...] = src[...].astype(BF16)

    @pl.when(k == 0)
    def _():
        out_ref[...] = jnp.zeros_like(out_ref)

    @pl.when(k < SLAB_SPLIT)
    def _():
        for src, dst in zip(cast_in[n_tile_casts:], cast_out[n_tile_casts:]):
            dst[...] = src[...].astype(BF16)

    h = h_ref[...]
    gate = jnp.dot(h, wg_ref[...], preferred_element_type=F32)
    up = jnp.dot(h, wu_ref[...], preferred_element_type=F32)
    act = (gate * jax.nn.sigmoid(gate) * up).astype(BF16)
    out_ref[...] += jnp.dot(act, wd_ref[...], preferred_element_type=F32)

    @pl.when(k == pl.num_programs(1) - 1)
    def _():
        out_ref[...] = _rms(out_ref[...], gpost_ref[...])


def _ffn(h, wg, wu, wd, gpost, next_ffn=None, slab_casts=()):
    n, dm = h.shape
    hid = wg.shape[2]
    grid = (n // FFN_TM, hid // FFN_TH)
    assert grid[1] >= SLAB_SPLIT
    row = lambda i, k: (i, 0)
    fixed = lambda i, k: (0, 0)
    in_specs = [
        pl.BlockSpec((FFN_TM, dm), row),
        pl.BlockSpec((None, dm, FFN_TH), lambda i, k: (0, 0, k)),
        pl.BlockSpec((None, dm, FFN_TH), lambda i, k: (0, 0, k)),
        pl.BlockSpec((None, FFN_TH, dm), lambda i, k: (0, k, 0)),
        pl.BlockSpec((1, dm), fixed),
    ]
    args = [h, wg, wu, wd, gpost]
    out_shape = [jax.ShapeDtypeStruct((n, dm), F32)]
    out_specs = [pl.BlockSpec((FFN_TM, dm), row)]

    def add_cast(stack, block, in_map, out_map):
        in_specs.append(pl.BlockSpec((None,) + block, in_map))
        args.append(stack)
        out_shape.append(jax.ShapeDtypeStruct((1,) + stack.shape[1:], BF16))
        out_specs.append(pl.BlockSpec((None,) + block, out_map))

    n_tile_casts = 0
    if next_ffn is not None:
        ng, nu, nd, layer = next_ffn
        cr = dm // grid[0]
        assert dm % grid[0] == 0 and cr % 8 == 0
        add_cast(ng, (cr, FFN_TH), lambda i, k: (layer, i, k), lambda i, k: (0, i, k))
        add_cast(nu, (cr, FFN_TH), lambda i, k: (layer, i, k), lambda i, k: (0, i, k))
        add_cast(nd, (FFN_TH, cr), lambda i, k: (layer, k, i), lambda i, k: (0, k, i))
        n_tile_casts = 3
    for stack, index in slab_casts:
        rows = stack.shape[1] // (grid[0] * SLAB_SPLIT)
        assert stack.shape[1] % (grid[0] * SLAB_SPLIT) == 0 and rows % 8 == 0
        slab = lambda i, k: SLAB_SPLIT * i + jnp.minimum(k, SLAB_SPLIT - 1)
        add_cast(stack, (rows, stack.shape[2]),
                 lambda i, k, index=index: (index, slab(i, k), 0), lambda i, k: (0, slab(i, k), 0))
    res = pl.pallas_call(
        functools.partial(_ffn_kernel, n_tile_casts=n_tile_casts, n_slab_casts=len(slab_casts)),
        out_shape=out_shape,
        grid=grid,
        in_specs=in_specs,
        out_specs=out_specs,
        compiler_params=_params("parallel", "arbitrary", vmem_limit_bytes=FFN_VMEM_LIMIT_BYTES),
        name="ffn",
    )(*args)
    return res[0], tuple(res[1:1 + n_tile_casts]), tuple(res[1 + n_tile_casts:])


def _add_kernel(a_ref, b_ref, o_ref):
    o_ref[...] = a_ref[...] + b_ref[...]


def _add(a, b):
    n, dm = a.shape
    spec = pl.BlockSpec((FFN_TM, dm), lambda i: (i, 0))
    return pl.pallas_call(
        _add_kernel,
        out_shape=jax.ShapeDtypeStruct((n, dm), F32),
        grid=(n // FFN_TM,),
        in_specs=[spec, spec],
        out_specs=spec,
        compiler_params=_params("parallel"),
        name="residual_add",
    )(a, b)


def _local_attn_body(slope, sink, q_ref, k_ref, v_ref, o_ref, lse_ref, bm_ref, scratch, *, half, dil):
    seq = q_ref.shape[0]
    bq = LOCAL_BQ
    lc = seq // dil
    ks = min(3 * bq, lc)
    nqb = lc // bq
    if dil > 1:
        qf, kf, vf, of = scratch
        qf[...] = q_ref[...].astype(F32)
        kf[...] = k_ref[...].astype(F32)
        vf[...] = v_ref[...].astype(F32)
    rel0 = (lax.broadcasted_iota(jnp.int32, (bq, ks), 0)
            - lax.broadcasted_iota(jnp.int32, (bq, ks), 1))
    step_slope = slope * (dil * LOG2E)
    for case in range(3):
        dist = jnp.abs(rel0 + case * bq)
        bm_ref[case, :, :ks] = jnp.where(dist <= half, -step_slope * dist.astype(F32), NEG_INF)
    sink2 = None if sink is None else sink * LOG2E

    def logits(idx):
        r = idx // nqb
        q0 = (idx % nqb) * bq
        k0 = jnp.clip(q0 - bq, 0, lc - ks)
        if dil == 1:
            rows = pl.ds(pl.multiple_of(q0, bq), bq)
            q = q_ref[rows, :]
            k = k_ref[pl.ds(pl.multiple_of(k0, bq), ks), :]
            v = v_ref[pl.ds(pl.multiple_of(k0, bq), ks), :]
        else:
            rows = pl.ds(r + dil * q0, bq, stride=dil)
            q = qf[rows, :].astype(BF16)
            k = kf[pl.ds(r + dil * k0, ks, stride=dil), :].astype(BF16)
            v = vf[pl.ds(r + dil * k0, ks, stride=dil), :].astype(BF16)
        s = lax.dot_general(q, k, (((1,), (1,)), ((), ())), preferred_element_type=F32)
        return s + bm_ref[(q0 - k0) // bq, :, :ks], v, rows

    def softmax(s):
        mx = jnp.max(s, axis=-1, keepdims=True)
        if sink2 is not None:
            mx = jnp.maximum(mx, sink2)
        e = jnp.exp2(s - mx)
        den = jnp.sum(e, axis=-1, keepdims=True)
        if sink2 is not None:
            den = den + jnp.exp2(sink2 - mx)
        return e.astype(BF16), mx, den

    def finish(e, mx, den, v, rows):
        o = jnp.dot(e, v, preferred_element_type=F32) / den
        if dil == 1:
            o_ref[rows, :] = o.astype(o_ref.dtype)
        else:
            of[rows, :] = o
        if lse_ref is not None:
            lse = mx * (1.0 / LOG2E) + jnp.log(den)
            lse_ref[rows, :] = jnp.broadcast_to(lse, (bq, HEAD_DIM))

    def group(gidx, carry):
        staged = [logits(gidx * LOCAL_GROUP + u) for u in range(LOCAL_GROUP)]
        probs = [softmax(s) for s, _, _ in staged]
        for (e, mx, den), (_, v, rows) in zip(probs, staged):
            finish(e, mx, den, v, rows)
        return carry

    lax.fori_loop(0, dil * nqb // LOCAL_GROUP, group, 0)
    if dil > 1:
        o_ref[...] = of[...].astype(o_ref.dtype)


def _window_attn_kernel(slope_ref, sink_ref, q_ref, k_ref, v_ref, o_ref, bm_ref):
    head = pl.program_id(1)
    _local_attn_body(slope_ref[head], sink_ref[head], q_ref, k_ref, v_ref, o_ref, None, bm_ref, None,
                     half=WINDOW_A, dil=1)


def _dilated_attn_kernel(slope_ref, q_ref, k_ref, v_ref, o_ref, lse_ref, bm_ref, *scratch):
    head = pl.program_id(1)
    h0 = 0
    for (window, dil), hg in zip(DILATED_PATTERNS, B_GROUP_HEADS):
        @pl.when((head >= h0) & (head < h0 + hg))
        def _(window=window, dil=dil):
            _local_attn_body(slope_ref[head], None, q_ref, k_ref, v_ref, o_ref, lse_ref, bm_ref, scratch,
                             half=window // (2 * dil), dil=dil)
        h0 += hg


_BIAS_TILES = pltpu.VMEM((3, LOCAL_BQ, 3 * LOCAL_BQ), F32)


def _head_spec(seq, col0, group=1):
    return pl.BlockSpec((None, seq, HEAD_DIM), lambda b, h: (b, 0, col0 + h // group))


def _window_attn(qkv, slopes, sink):
    bsz, seq, _ = qkv.shape
    smem = pl.BlockSpec(memory_space=pltpu.SMEM)
    return pl.pallas_call(
        _window_attn_kernel,
        out_shape=jax.ShapeDtypeStruct((bsz, seq, N_HEADS * HEAD_DIM), BF16),
        grid=(bsz, N_HEADS),
        in_specs=[smem, smem, _head_spec(seq, 0), _head_spec(seq, N_HEADS, GQA_GROUP),
                  _head_spec(seq, N_HEADS + N_KV_HEADS, GQA_GROUP)],
        out_specs=_head_spec(seq, 0),
        scratch_shapes=[_BIAS_TILES],
        compiler_params=_params("parallel", "arbitrary"),
        name="window_attn",
    )(slopes, sink, qkv, qkv, qkv)


def _dilated_attn(qkv, slopes):
    bsz, seq, _ = qkv.shape
    width = N_HEADS * HEAD_DIM
    return pl.pallas_call(
        _dilated_attn_kernel,
        out_shape=[jax.ShapeDtypeStruct((bsz, seq, width), BF16),
                   jax.ShapeDtypeStruct((bsz, seq, width), F32)],
        grid=(bsz, N_HEADS),
        in_specs=[pl.BlockSpec(memory_space=pltpu.SMEM), _head_spec(seq, 0),
                  _head_spec(seq, N_HEADS), _head_spec(seq, 2 * N_HEADS)],
        out_specs=[_head_spec(seq, 0), _head_spec(seq, 0)],
        scratch_shapes=[_BIAS_TILES] + [pltpu.VMEM((seq, HEAD_DIM), F32) for _ in range(4)],
        compiler_params=_params("parallel", "arbitrary"),
        name="dilated_attn",
    )(slopes, qkv, qkv, qkv)


AUG_SIGN, AUG_HI, AUG_LO = 0, N_AUG, 2 * N_AUG


def _split_bf16(x, n=N_AUG):
    terms, rest = [], np.asarray(x, np.float64)
    for _ in range(n):
        t = rest.astype(ml_dtypes.bfloat16).astype(np.float64)
        terms.append(t)
        rest = rest - t
    return terms


def _diff_tables(seq):
    bq = DIFF_BQ
    m2 = _alibi_slopes_np() * LOG2E
    t = np.arange(bq, dtype=np.float64)
    qaug = np.zeros((N_KV_HEADS, 2, GQA_GROUP, bq, HEAD_DIM), np.float64)
    dbias = np.zeros((N_KV_HEADS, 2, GQA_GROUP, bq, bq), np.float64)
    for kvh in range(N_KV_HEADS):
        for g in range(GQA_GROUP):
            m = m2[kvh * GQA_GROUP + g]
            for i, term in enumerate(_split_bf16(m * t)):
                qaug[kvh, :, g, :, AUG_SIGN + i] = term
            for i, term in enumerate(_split_bf16(-float(bq) * m)):
                qaug[kvh, :, g, :, AUG_HI + i] = term
            for i, term in enumerate(_split_bf16(-m)):
                qaug[kvh, :, g, :, AUG_LO + i] = term
            dbias[kvh, :, g] = -m * np.abs(t[:, None] - t[None, :])
    rows = 2 * GQA_GROUP * bq
    qaug = qaug.reshape(N_KV_HEADS, rows, HEAD_DIM).astype(ml_dtypes.bfloat16)
    dbias = dbias.reshape(N_KV_HEADS, rows, bq).astype(np.float32)
    rel = np.arange(-seq, seq, dtype=np.int64)
    hi = np.floor_divide(rel, bq)
    lo = rel - hi * bq
    sign = np.where(rel >= bq, 1, np.where(rel < 0, -1, 0))
    kaug = np.zeros((2 * seq, HEAD_DIM), np.float64)
    for i in range(N_AUG):
        kaug[:, AUG_SIGN + i] = sign
        kaug[:, AUG_HI + i] = sign * hi
        kaug[:, AUG_LO + i] = sign * lo
    return jnp.asarray(qaug), jnp.asarray(dbias), jnp.asarray(kaug.astype(ml_dtypes.bfloat16))


def _lane_tiles(x):
    return [x[:, j * HEAD_DIM:(j + 1) * HEAD_DIM] for j in range(x.shape[1] // HEAD_DIM)]


def _diff_attn_kernel(lq1_ref, lk1_ref, lq2_ref, lk2_ref, sg_ref, qaug_ref, dbias_ref, kaug_ref,
                      q_ref, k_ref, v_ref, o_ref,
                      kp_ref, qp_ref, l_ref, mx_ref, den_ref, acc_ref, *, lambda_init):
    bq = DIFF_BQ
    seq = k_ref.shape[0]
    nchunk = seq // DIFF_KC
    n = pl.program_id(2)
    t0 = n * bq
    dn = (((1,), (1,)), ((), ()))

    @pl.when(n == 0)
    def _():
        kp_ref[:, :HEAD_DIM] = k_ref[...]

    kp_ref[:, HEAD_DIM:] = kaug_ref[pl.ds(pl.multiple_of(seq - t0, bq), seq), :]

    lane = lax.broadcasted_iota(jnp.int32, (bq, HEAD_DIM), 1)
    for c in range(2):
        keep = (lane < DIFF_HEAD_DIM) if c == 0 else (lane >= DIFF_HEAD_DIM)
        for g in range(GQA_GROUP):
            qg = q_ref[:, g * HEAD_DIM:(g + 1) * HEAD_DIM]
            r0 = (c * GQA_GROUP + g) * bq
            qp_ref[r0:r0 + bq, :HEAD_DIM] = jnp.where(keep, qg, jnp.zeros_like(qg))
    qp_ref[:, HEAD_DIM:] = qaug_ref[...]

    tiles_per_chunk = DIFF_KC // HEAD_DIM
    for c in range(nchunk):
        l = lax.dot_general(qp_ref[...], kp_ref[c * DIFF_KC:(c + 1) * DIFF_KC, :], dn,
                            preferred_element_type=F32)
        tiles = []
        for j, lt in enumerate(_lane_tiles(l)):
            own = (c * tiles_per_chunk + j) == n
            tiles.append(lt + jnp.where(own, dbias_ref[...], 0.0))
        l_ref[c] = jnp.concatenate(tiles, axis=-1)
        m = jnp.maximum(jnp.maximum(tiles[0], tiles[1]), jnp.maximum(tiles[2], tiles[3]))
        mx_ref[...] = m if c == 0 else jnp.maximum(mx_ref[...], m)
    mx_ref[...] = jnp.broadcast_to(jnp.max(mx_ref[...], axis=-1, keepdims=True), mx_ref.shape)
    den_ref[...] = jnp.zeros_like(den_ref)
    acc_ref[...] = jnp.zeros_like(acc_ref)

    def pv_chunk(c, carry):
        mxb = mx_ref[...]
        es = [jnp.exp2(lt - mxb) for lt in _lane_tiles(l_ref[c])]
        den_ref[...] += (es[0] + es[1]) + (es[2] + es[3])
        e = jnp.concatenate([x.astype(BF16) for x in es], axis=-1)
        vc = v_ref[pl.ds(pl.multiple_of(c * DIFF_KC, DIFF_KC), DIFF_KC), :]
        acc_ref[...] += jnp.dot(e, vc, preferred_element_type=F32)
        return carry

    lax.fori_loop(0, nchunk, pv_chunk, 0, unroll=DIFF_UNROLL)

    lam = (jnp.exp(jnp.sum(lq1_ref[...] * lk1_ref[...], axis=-1, keepdims=True))
           - jnp.exp(jnp.sum(lq2_ref[...] * lk2_ref[...], axis=-1, keepdims=True))
           + lambda_init)
    p = acc_ref[...] / jnp.sum(den_ref[...], axis=-1, keepdims=True)
    half_rows = GQA_GROUP * bq
    o = p[:half_rows] - lam * p[half_rows:]
    for g in range(GQA_GROUP):
        og = _rms(o[g * bq:(g + 1) * bq], sg_ref[...]) * (1.0 - lambda_init)
        o_ref[:, g * HEAD_DIM:(g + 1) * HEAD_DIM] = og.astype(o_ref.dtype)


def _diff_attn(qkv, lq1, lk1, lq2, lk2, subln_g, lambda_init):
    bsz, seq, _ = qkv.shape
    assert DIFF_KC == 4 * HEAD_DIM and seq % DIFF_KC == 0
    bq = DIFF_BQ
    rows = 2 * GQA_GROUP * bq
    gw = GQA_GROUP * HEAD_DIM
    qaug, dbias, kaug = _diff_tables(seq)
    vec = lambda w: pl.BlockSpec((1, w), lambda b, h, n: (0, 0))
    per_head = lambda w: pl.BlockSpec((None, rows, w), lambda b, h, n: (h, 0, 0))
    q_spec = pl.BlockSpec((None, bq, gw), lambda b, h, n: (b, n, h))
    k_spec = pl.BlockSpec((None, seq, HEAD_DIM), lambda b, h, n: (b, 0, N_HEADS + h))
    v_spec = pl.BlockSpec((None, seq, HEAD_DIM), lambda b, h, n: (b, 0, N_HEADS + N_KV_HEADS + h))
    return pl.pallas_call(
        functools.partial(_diff_attn_kernel, lambda_init=lambda_init),
        out_shape=jax.ShapeDtypeStruct((bsz, seq, N_HEADS * HEAD_DIM), BF16),
        grid=(bsz, N_KV_HEADS, seq // bq),
        in_specs=[vec(DIFF_HEAD_DIM), vec(DIFF_HEAD_DIM), vec(DIFF_HEAD_DIM), vec(DIFF_HEAD_DIM),
                  vec(HEAD_DIM), per_head(HEAD_DIM), per_head(bq),
                  pl.BlockSpec((2 * seq, HEAD_DIM), lambda b, h, n: (0, 0)),
                  q_spec, k_spec, v_spec],
        out_specs=q_spec,
        scratch_shapes=[
            pltpu.VMEM((seq, 2 * HEAD_DIM), BF16),
            pltpu.VMEM((rows, 2 * HEAD_DIM), BF16),
            pltpu.VMEM((seq // DIFF_KC, rows, DIFF_KC), F32),
            pltpu.VMEM((rows, HEAD_DIM), F32),
            pltpu.VMEM((rows, HEAD_DIM), F32),
            pltpu.VMEM((rows, HEAD_DIM), F32),
        ],
        compiler_params=_params("parallel", "arbitrary", "arbitrary"),
        name="diff_attn",
    )(lq1, lk1, lq2, lk2, subln_g, qaug, dbias, kaug, qkv, qkv, qkv)


def kernel(x, mix_pre_g, mix_post_g, ffn_pre_g, ffn_post_g, w_gate, w_up, w_down, a_w_qkv, a_w_o, a_sink, b_w_qkv, b_w_o, c_w_qkv, c_w_o, c_lambda_q1, c_lambda_k1, c_lambda_q2, c_lambda_k2, c_subln_g):
    bsz, seq, dm = x.shape
    n = bsz * seq
    slopes = jnp.asarray(_alibi_slopes_np(), F32)
    xf = x.reshape(n, dm)
    row = lambda a, i: a[i].reshape(1, -1)
    qkv_stacks, o_stacks = (a_w_qkv, b_w_qkv, c_w_qkv), (a_w_o, b_w_o, c_w_o)
    w_qkv, w_o = a_w_qkv[:1].astype(BF16), a_w_o[:1].astype(BF16)
    ffn_w = tuple(w[:1].astype(BF16) for w in (w_gate, w_up, w_down))
    q_cols = N_HEADS * HEAD_DIM
    delta = None

    for i in range(DEPTH):
        m, j = i % N_MIXERS, i // N_MIXERS
        pre = row(mix_pre_g, i)
        head_dim = DIFF_HEAD_DIM if m == 2 else HEAD_DIM
        qkv = _norm_matmul(xf, delta, pre, w_qkv, q_cols, LOG2E * head_dim ** -0.5).reshape(bsz, seq, -1)
        lse = None
        if m == 0:
            o = _window_attn(qkv, slopes, a_sink[j])
        elif m == 1:
            o, lse = _dilated_attn(qkv, slopes)
            lse = lse.reshape(n, dm)
        else:
            lambda_init = 0.8 - 0.6 * math.exp(-0.3 * i)
            o = _diff_attn(qkv, row(c_lambda_q1, j), row(c_lambda_k1, j), row(c_lambda_q2, j),
                           row(c_lambda_k2, j), row(c_subln_g, j), lambda_init)
        xf, h = _out_proj(o.reshape(n, dm), w_o, row(mix_post_g, i), row(ffn_pre_g, i), xf, delta, lse=lse)
        if i + 1 < DEPTH:
            m1, j1 = (i + 1) % N_MIXERS, (i + 1) // N_MIXERS
            delta, ffn_w, (w_qkv, w_o) = _ffn(h, *ffn_w, row(ffn_post_g, i),
                                              next_ffn=(w_gate, w_up, w_down, i + 1),
                                              slab_casts=((qkv_stacks[m1], j1), (o_stacks[m1], j1)))
        else:
            delta, _, _ = _ffn(h, *ffn_w, row(ffn_post_g, i))
    xf = _add(xf, delta)
    return xf.reshape(bsz, seq, dm)
```

```python
import functools
import math

import jax
import jax.numpy as jnp
import ml_dtypes
import numpy as np
from jax import lax
from jax.experimental import pallas as pl
from jax.experimental.pallas import tpu as pltpu

F32 = jnp.float32
BF16 = jnp.bfloat16

D_MODEL = 2048
DEPTH = 4
N_MIXERS = 3
HEAD_DIM = 128
N_HEADS = 16
N_KV_HEADS = 4
GQA_GROUP = N_HEADS // N_KV_HEADS
DIFF_HEAD_DIM = HEAD_DIM // 2
WINDOW_A = 128
DILATED_PATTERNS = ((128, 1), (512, 4), (2048, 16))
B_GROUP_HEADS = (6, 5, 5)
FFN_HIDDEN = 5632
RMS_EPS = 1e-6
NEG_INF = -1e30
LOG2E = math.log2(math.e)

VMEM_LIMIT_BYTES = 48 * 1024 * 1024
WIDE_VMEM_LIMIT_BYTES = 60 * 1024 * 1024
FFN_VMEM_LIMIT_BYTES = WIDE_VMEM_LIMIT_BYTES

PROJ_TM, PROJ_TN, PROJ_TN_RESIDUAL = 1024, 1024, 1024
NORM_ROWS = 256
OUT_TM = 512
OUT_ROW_CHUNKS = 4
FFN_TM, FFN_TH = 1024, 512
SLAB_SPLIT = 4
LOCAL_BQ = 128
LOCAL_GROUP = 8
DIFF_BQ = 128
DIFF_KC = 512
DIFF_UNROLL = 4
N_AUG = 3


def _alibi_slopes_np():
    return 2.0 ** (-8.0 * np.arange(1, N_HEADS + 1, dtype=np.float64) / N_HEADS)


def _rms(x, g):
    return x * lax.rsqrt(jnp.mean(x * x, axis=-1, keepdims=True) + RMS_EPS) * g


def _params(*sem, vmem_limit_bytes=VMEM_LIMIT_BYTES):
    return pltpu.CompilerParams(dimension_semantics=sem, vmem_limit_bytes=vmem_limit_bytes)


def _norm_matmul_kernel(*refs, scaled_tiles, scale, has_delta):
    if has_delta:
        x_ref, d_ref, g_ref, w_ref, o_ref, h_ref = refs
    else:
        x_ref, g_ref, w_ref, o_ref, h_ref = refs
    j = pl.program_id(1)

    @pl.when(j == 0)
    def _():
        def norm_rows(r, carry):
            rows = pl.ds(pl.multiple_of(r * NORM_ROWS, NORM_ROWS), NORM_ROWS)
            x = x_ref[rows, :]
            if has_delta:
                x = x + d_ref[rows, :]
            h_ref[rows, :] = _rms(x, g_ref[...]).astype(BF16)
            return carry

        lax.fori_loop(0, x_ref.shape[0] // NORM_ROWS, norm_rows, 0)

    acc = jnp.dot(h_ref[...], w_ref[...], preferred_element_type=F32)
    if scaled_tiles:
        acc = acc * jnp.where(j < scaled_tiles, scale, 1.0)
    o_ref[...] = acc.astype(o_ref.dtype)


def _norm_matmul(x, delta, g, w, scaled_cols=0, scale=1.0):
    n, dm = x.shape
    m = w.shape[2]
    has_delta = delta is not None
    tn = PROJ_TN_RESIDUAL if has_delta else PROJ_TN
    assert scaled_cols % tn == 0
    x_spec = pl.BlockSpec((PROJ_TM, dm), lambda i, j: (i, 0))
    return pl.pallas_call(
        functools.partial(_norm_matmul_kernel, scaled_tiles=scaled_cols // tn, scale=scale,
                          has_delta=has_delta),
        out_shape=jax.ShapeDtypeStruct((n, m), BF16),
        grid=(n // PROJ_TM, m // tn),
        in_specs=[x_spec] * (1 + has_delta) + [
            pl.BlockSpec((1, dm), lambda i, j: (0, 0)),
            pl.BlockSpec((None, dm, tn), lambda i, j: (0, 0, j)),
        ],
        out_specs=pl.BlockSpec((PROJ_TM, tn), lambda i, j: (i, j)),
        scratch_shapes=[pltpu.VMEM((PROJ_TM, dm), BF16)],
        compiler_params=_params("parallel", "arbitrary", vmem_limit_bytes=WIDE_VMEM_LIMIT_BYTES),
        name="norm_matmul",
    )(*([x] + [delta] * has_delta + [g, w]))


def _out_tail(y, rows, g_ref, gn_ref, x_ref, rest):
    out_ref, hn_ref = rest[-2:]
    x = x_ref[rows, :]
    if len(rest) == 3:
        x = x + rest[0][rows, :]
    x1 = x + _rms(y, g_ref[...])
    out_ref[rows, :] = x1
    hn_ref[rows, :] = _rms(x1, gn_ref[...]).astype(BF16)


def _out_kernel(o_ref, w_ref, g_ref, gn_ref, x_ref, *rest):
    chunk = o_ref.shape[0] // OUT_ROW_CHUNKS
    for r0 in range(0, o_ref.shape[0], chunk):
        rows = slice(r0, r0 + chunk)
        y = jnp.dot(o_ref[rows, :], w_ref[...], preferred_element_type=F32)
        _out_tail(y, rows, g_ref, gn_ref, x_ref, rest)


def _out_merge_kernel(o_ref, lse_ref, w_ref, g_ref, gn_ref, x_ref, *rest):
    scores = []
    h0 = 0
    for hg in B_GROUP_HEADS:
        acc = lse_ref[:, h0 * HEAD_DIM:(h0 + 1) * HEAD_DIM]
        for h in range(h0 + 1, h0 + hg):
            acc = acc + lse_ref[:, h * HEAD_DIM:(h + 1) * HEAD_DIM]
        scores.append(acc * (1.0 / hg))
        h0 += hg
    mx = jnp.maximum(jnp.maximum(scores[0], scores[1]), scores[2])
    es = [jnp.exp(s - mx) for s in scores]
    inv = 1.0 / (es[0] + es[1] + es[2])
    pieces = []
    h0 = 0
    for gi, hg in enumerate(B_GROUP_HEADS):
        alpha = es[gi] * inv
        for h in range(h0, h0 + hg):
            oh = o_ref[:, h * HEAD_DIM:(h + 1) * HEAD_DIM].astype(F32)
            pieces.append((oh * alpha).astype(BF16))
        h0 += hg
    mixed = jnp.concatenate(pieces, axis=-1)
    y = jnp.dot(mixed, w_ref[...], preferred_element_type=F32)
    _out_tail(y, slice(None), g_ref, gn_ref, x_ref, rest)


def _out_proj(o, w, g, g_next, x, delta=None, lse=None):
    n, dm = x.shape
    row = lambda i: (i, 0)
    fixed = lambda i: (0, 0)
    o_spec = pl.BlockSpec((OUT_TM, dm), row)
    w_spec = pl.BlockSpec((None, dm, dm), lambda i: (0, 0, 0))
    g_spec = pl.BlockSpec((1, dm), fixed)
    if lse is None:
        body, args, in_specs = _out_kernel, [o], [o_spec]
    else:
        body, args, in_specs = _out_merge_kernel, [o, lse], [o_spec, o_spec]
    args += [w, g, g_next, x] + ([] if delta is None else [delta])
    in_specs += [w_spec, g_spec, g_spec, o_spec] + ([] if delta is None else [o_spec])
    return pl.pallas_call(
        body,
        out_shape=[jax.ShapeDtypeStruct((n, dm), F32), jax.ShapeDtypeStruct((n, dm), BF16)],
        grid=(n // OUT_TM,),
        in_specs=in_specs,
        out_specs=[o_spec, o_spec],
        compiler_params=_params("parallel", vmem_limit_bytes=WIDE_VMEM_LIMIT_BYTES),
        name="out_proj",
    )(*args)


def _ffn_kernel(h_ref, wg_ref, wu_ref, wd_ref, gpost_ref, *rest, n_tile_casts, n_slab_casts):
    n_casts = n_tile_casts + n_slab_casts
    cast_in, out_ref, cast_out = rest[:n_casts], rest[n_casts], rest[n_casts + 1:]
    k = pl.program_id(1)

    for src, dst in zip(cast_in[:n_tile_casts], cast_out[:n_tile_casts]):
        dst[...] = src[...].astype(BF16)

    @pl.when(k == 0)
    def _():
        out_ref[...] = jnp.zeros_like(out_ref)

    @pl.when(k < SLAB_SPLIT)
    def _():
        for src, dst in zip(cast_in[n_tile_casts:], cast_out[n_tile_casts:]):
            dst[...] = src[...].astype(BF16)

    h = h_ref[...]
    gate = jnp.dot(h, wg_ref[...], preferred_element_type=F32)
    up = jnp.dot(h, wu_ref[...], preferred_element_type=F32)
    act = (gate * jax.nn.sigmoid(gate) * up).astype(BF16)
    out_ref[...] += jnp.dot(act, wd_ref[...], preferred_element_type=F32)

    @pl.when(k == pl.num_programs(1) - 1)
    def _():
        out_ref[...] = _rms(out_ref[...], gpost_ref[...])


def _ffn(h, wg, wu, wd, gpost, next_ffn=None, slab_casts=()):
    n, dm = h.shape
    hid = wg.shape[2]
    grid = (n // FFN_TM, hid // FFN_TH)
    assert grid[1] >= SLAB_SPLIT
    row = lambda i, k: (i, 0)
    fixed = lambda i, k: (0, 0)
    in_specs = [
        pl.BlockSpec((FFN_TM, dm), row),
        pl.BlockSpec((None, dm, FFN_TH), lambda i, k: (0, 0, k)),
        pl.BlockSpec((None, dm, FFN_TH), lambda i, k: (0, 0, k)),
        pl.BlockSpec((None, FFN_TH, dm), lambda i, k: (0, k, 0)),
        pl.BlockSpec((1, dm), fixed),
    ]
    args = [h, wg, wu, wd, gpost]
    out_shape = [jax.ShapeDtypeStruct((n, dm), F32)]
    out_specs = [pl.BlockSpec((FFN_TM, dm), row)]

    def add_cast(stack, block, in_map, out_map):
        in_specs.append(pl.BlockSpec((None,) + block, in_map))
        args.append(stack)
        out_shape.append(jax.ShapeDtypeStruct((1,) + stack.shape[1:], BF16))
        out_specs.append(pl.BlockSpec((None,) + block, out_map))

    n_tile_casts = 0
    if next_ffn is not None:
        ng, nu, nd, layer = next_ffn
        cr = dm // grid[0]
        assert dm % grid[0] == 0 and cr % 8 == 0
        add_cast(ng, (cr, FFN_TH), lambda i, k: (layer, i, k), lambda i, k: (0, i, k))
        add_cast(nu, (cr, FFN_TH), lambda i, k: (layer, i, k), lambda i, k: (0, i, k))
        add_cast(nd, (FFN_TH, cr), lambda i, k: (layer, k, i), lambda i, k: (0, k, i))
        n_tile_casts = 3
    for stack, index in slab_casts:
        rows = stack.shape[1] // (grid[0] * SLAB_SPLIT)
        assert stack.shape[1] % (grid[0] * SLAB_SPLIT) == 0 and rows % 8 == 0
        slab = lambda i, k: SLAB_SPLIT * i + jnp.minimum(k, SLAB_SPLIT - 1)
        add_cast(stack, (rows, stack.shape[2]),
                 lambda i, k, index=index: (index, slab(i, k), 0), lambda i, k: (0, slab(i, k), 0))
    res = pl.pallas_call(
        functools.partial(_ffn_kernel, n_tile_casts=n_tile_casts, n_slab_casts=len(slab_casts)),
        out_shape=out_shape,
        grid=grid,
        in_specs=in_specs,
        out_specs=out_specs,
        compiler_params=_params("parallel", "arbitrary", vmem_limit_bytes=FFN_VMEM_LIMIT_BYTES),
        name="ffn",
    )(*args)
    return res[0], tuple(res[1:1 + n_tile_casts]), tuple(res[1 + n_tile_casts:])


def _add_kernel(a_ref, b_ref, o_ref):
    o_ref[...] = a_ref[...] + b_ref[...]


def _add(a, b):
    n, dm = a.shape
    spec = pl.BlockSpec((FFN_TM, dm), lambda i: (i, 0))
    return pl.pallas_call(
        _add_kernel,
        out_shape=jax.ShapeDtypeStruct((n, dm), F32),
        grid=(n // FFN_TM,),
        in_specs=[spec, spec],
        out_specs=spec,
        compiler_params=_params("parallel"),
        name="residual_add",
    )(a, b)


def _local_attn_body(slope, sink, q_ref, k_ref, v_ref, o_ref, lse_ref, bm_ref, scratch, *, half, dil):
    seq = q_ref.shape[0]
    bq = LOCAL_BQ
    lc = seq // dil
    ks = min(3 * bq, lc)
    nqb = lc // bq
    if dil > 1:
        qf, kf, vf, of = scratch
        qf[...] = q_ref[...].astype(F32)
        kf[...] = k_ref[...].astype(F32)
        vf[...] = v_ref[...].astype(F32)
    rel0 = (lax.broadcasted_iota(jnp.int32, (bq, ks), 0)
            - lax.broadcasted_iota(jnp.int32, (bq, ks), 1))
    step_slope = slope * (dil * LOG2E)
    for case in range(3):
        dist = jnp.abs(rel0 + case * bq)
        bm_ref[case, :, :ks] = jnp.where(dist <= half, -step_slope * dist.astype(F32), NEG_INF)
    sink2 = None if sink is None else sink * LOG2E

    def logits(idx):
        r = idx // nqb
        q0 = (idx % nqb) * bq
        k0 = jnp.clip(q0 - bq, 0, lc - ks)
        if dil == 1:
            rows = pl.ds(pl.multiple_of(q0, bq), bq)
            q = q_ref[rows, :]
            k = k_ref[pl.ds(pl.multiple_of(k0, bq), ks), :]
            v = v_ref[pl.ds(pl.multiple_of(k0, bq), ks), :]
        else:
            rows = pl.ds(r + dil * q0, bq, stride=dil)
            q = qf[rows, :].astype(BF16)
            k = kf[pl.ds(r + dil * k0, ks, stride=dil), :].astype(BF16)
            v = vf[pl.ds(r + dil * k0, ks, stride=dil), :].astype(BF16)
        s = lax.dot_general(q, k, (((1,), (1,)), ((), ())), preferred_element_type=F32)
        return s + bm_ref[(q0 - k0) // bq, :, :ks], v, rows

    def softmax(s):
        mx = jnp.max(s, axis=-1, keepdims=True)
        if sink2 is not None:
            mx = jnp.maximum(mx, sink2)
        e = jnp.exp2(s - mx)
        den = jnp.sum(e, axis=-1, keepdims=True)
        if sink2 is not None:
            den = den + jnp.exp2(sink2 - mx)
        return e.astype(BF16), mx, den

    def finish(e, mx, den, v, rows):
        o = jnp.dot(e, v, preferred_element_type=F32) / den
        if dil == 1:
            o_ref[rows, :] = o.astype(o_ref.dtype)
        else:
            of[rows, :] = o
        if lse_ref is not None:
            lse = mx * (1.0 / LOG2E) + jnp.log(den)
            lse_ref[rows, :] = jnp.broadcast_to(lse, (bq, HEAD_DIM))

    def group(gidx, carry):
        staged = [logits(gidx * LOCAL_GROUP + u) for u in range(LOCAL_GROUP)]
        probs = [softmax(s) for s, _, _ in staged]
        for (e, mx, den), (_, v, rows) in zip(probs, staged):
            finish(e, mx, den, v, rows)
        return carry

    lax.fori_loop(0, dil * nqb // LOCAL_GROUP, group, 0)
    if dil > 1:
        o_ref[...] = of[...].astype(o_ref.dtype)


def _window_attn_kernel(slope_ref, sink_ref, q_ref, k_ref, v_ref, o_ref, bm_ref):
    head = pl.program_id(1)
    _local_attn_body(slope_ref[head], sink_ref[head], q_ref, k_ref, v_ref, o_ref, None, bm_ref, None,
                     half=WINDOW_A, dil=1)


def _dilated_attn_kernel(slope_ref, q_ref, k_ref, v_ref, o_ref, lse_ref, bm_ref, *scratch):
    head = pl.program_id(1)
    h0 = 0
    for (window, dil), hg in zip(DILATED_PATTERNS, B_GROUP_HEADS):
        @pl.when((head >= h0) & (head < h0 + hg))
        def _(window=window, dil=dil):
            _local_attn_body(slope_ref[head], None, q_ref, k_ref, v_ref, o_ref, lse_ref, bm_ref, scratch,
                             half=window // (2 * dil), dil=dil)
        h0 += hg


_BIAS_TILES = pltpu.VMEM((3, LOCAL_BQ, 3 * LOCAL_BQ), F32)


def _head_spec(seq, col0, group=1):
    return pl.BlockSpec((None, seq, HEAD_DIM), lambda b, h: (b, 0, col0 + h // group))


def _window_attn(qkv, slopes, sink):
    bsz, seq, _ = qkv.shape
    smem = pl.BlockSpec(memory_space=pltpu.SMEM)
    return pl.pallas_call(
        _window_attn_kernel,
        out_shape=jax.ShapeDtypeStruct((bsz, seq, N_HEADS * HEAD_DIM), BF16),
        grid=(bsz, N_HEADS),
        in_specs=[smem, smem, _head_spec(seq, 0), _head_spec(seq, N_HEADS, GQA_GROUP),
                  _head_spec(seq, N_HEADS + N_KV_HEADS, GQA_GROUP)],
        out_specs=_head_spec(seq, 0),
        scratch_shapes=[_BIAS_TILES],
        compiler_params=_params("parallel", "arbitrary"),
        name="window_attn",
    )(slopes, sink, qkv, qkv, qkv)


def _dilated_attn(qkv, slopes):
    bsz, seq, _ = qkv.shape
    width = N_HEADS * HEAD_DIM
    return pl.pallas_call(
        _dilated_attn_kernel,
        out_shape=[jax.ShapeDtypeStruct((bsz, seq, width), BF16),
                   jax.ShapeDtypeStruct((bsz, seq, width), F32)],
        grid=(bsz, N_HEADS),
        in_specs=[pl.BlockSpec(memory_space=pltpu.SMEM), _head_spec(seq, 0),
                  _head_spec(seq, N_HEADS), _head_spec(seq, 2 * N_HEADS)],
        out_specs=[_head_spec(seq, 0), _head_spec(seq, 0)],
        scratch_shapes=[_BIAS_TILES] + [pltpu.VMEM((seq, HEAD_DIM), F32) for _ in range(4)],
        compiler_params=_params("parallel", "arbitrary"),
        name="dilated_attn",
    )(slopes, qkv, qkv, qkv)


AUG_SIGN, AUG_HI, AUG_LO = 0, N_AUG, 2 * N_AUG


def _split_bf16(x, n=N_AUG):
    terms, rest = [], np.asarray(x, np.float64)
    for _ in range(n):
        t = rest.astype(ml_dtypes.bfloat16).astype(np.float64)
        terms.append(t)
        rest = rest - t
    return terms


def _diff_tables(seq):
    bq = DIFF_BQ
    m2 = _alibi_slopes_np() * LOG2E
    t = np.arange(bq, dtype=np.float64)
    qaug = np.zeros((N_KV_HEADS, 2, GQA_GROUP, bq, HEAD_DIM), np.float64)
    dbias = np.zeros((N_KV_HEADS, 2, GQA_GROUP, bq, bq), np.float64)
    for kvh in range(N_KV_HEADS):
        for g in range(GQA_GROUP):
            m = m2[kvh * GQA_GROUP + g]
            for i, term in enumerate(_split_bf16(m * t)):
                qaug[kvh, :, g, :, AUG_SIGN + i] = term
            for i, term in enumerate(_split_bf16(-float(bq) * m)):
                qaug[kvh, :, g, :, AUG_HI + i] = term
            for i, term in enumerate(_split_bf16(-m)):
                qaug[kvh, :, g, :, AUG_LO + i] = term
            dbias[kvh, :, g] = -m * np.abs(t[:, None] - t[None, :])
    rows = 2 * GQA_GROUP * bq
    qaug = qaug.reshape(N_KV_HEADS, rows, HEAD_DIM).astype(ml_dtypes.bfloat16)
    dbias = dbias.reshape(N_KV_HEADS, rows, bq).astype(np.float32)
    rel = np.arange(-seq, seq, dtype=np.int64)
    hi = np.floor_divide(rel, bq)
    lo = rel - hi * bq
    sign = np.where(rel >= bq, 1, np.where(rel < 0, -1, 0))
    kaug = np.zeros((2 * seq, HEAD_DIM), np.float64)
    for i in range(N_AUG):
        kaug[:, AUG_SIGN + i] = sign
        kaug[:, AUG_HI + i] = sign * hi
        kaug[:, AUG_LO + i] = sign * lo
    return jnp.asarray(qaug), jnp.asarray(dbias), jnp.asarray(kaug.astype(ml_dtypes.bfloat16))


def _lane_tiles(x):
    return [x[:, j * HEAD_DIM:(j + 1) * HEAD_DIM] for j in range(x.shape[1] // HEAD_DIM)]


def _diff_attn_kernel(lq1_ref, lk1_ref, lq2_ref, lk2_ref, sg_ref, qaug_ref, dbias_ref, kaug_ref,
                      q_ref, k_ref, v_ref, o_ref,
                      kp_ref, qp_ref, l_ref, mx_ref, den_ref, acc_ref, *, lambda_init):
    bq = DIFF_BQ
    seq = k_ref.shape[0]
    nchunk = seq // DIFF_KC
    n = pl.program_id(2)
    t0 = n * bq
    dn = (((1,), (1,)), ((), ()))

    @pl.when(n == 0)
    def _():
        kp_ref[:, :HEAD_DIM] = k_ref[...]

    kp_ref[:, HEAD_DIM:] = kaug_ref[pl.ds(pl.multiple_of(seq - t0, bq), seq), :]

    lane = lax.broadcasted_iota(jnp.int32, (bq, HEAD_DIM), 1)
    for c in range(2):
        keep = (lane < DIFF_HEAD_DIM) if c == 0 else (lane >= DIFF_HEAD_DIM)
        for g in range(GQA_GROUP):
            qg = q_ref[:, g * HEAD_DIM:(g + 1) * HEAD_DIM]
            r0 = (c * GQA_GROUP + g) * bq
            qp_ref[r0:r0 + bq, :HEAD_DIM] = jnp.where(keep, qg, jnp.zeros_like(qg))
    qp_ref[:, HEAD_DIM:] = qaug_ref[...]

    tiles_per_chunk = DIFF_KC // HEAD_DIM
    for c in range(nchunk):
        l = lax.dot_general(qp_ref[...], kp_ref[c * DIFF_KC:(c + 1) * DIFF_KC, :], dn,
                            preferred_element_type=F32)
        tiles = []
        for j, lt in enumerate(_lane_tiles(l)):
            own = (c * tiles_per_chunk + j) == n
            tiles.append(lt + jnp.where(own, dbias_ref[...], 0.0))
        l_ref[c] = jnp.concatenate(tiles, axis=-1)
        m = jnp.maximum(jnp.maximum(tiles[0], tiles[1]), jnp.maximum(tiles[2], tiles[3]))
        mx_ref[...] = m if c == 0 else jnp.maximum(mx_ref[...], m)
    mx_ref[...] = jnp.broadcast_to(jnp.max(mx_ref[...], axis=-1, keepdims=True), mx_ref.shape)
    den_ref[...] = jnp.zeros_like(den_ref)
    acc_ref[...] = jnp.zeros_like(acc_ref)

    def pv_chunk(c, carry):
        mxb = mx_ref[...]
        es = [jnp.exp2(lt - mxb) for lt in _lane_tiles(l_ref[c])]
        den_ref[...] += (es[0] + es[1]) + (es[2] + es[3])
        e = jnp.concatenate([x.astype(BF16) for x in es], axis=-1)
        vc = v_ref[pl.ds(pl.multiple_of(c * DIFF_KC, DIFF_KC), DIFF_KC), :]
        acc_ref[...] += jnp.dot(e, vc, preferred_element_type=F32)
        return carry

    lax.fori_loop(0, nchunk, pv_chunk, 0, unroll=DIFF_UNROLL)

    lam = (jnp.exp(jnp.sum(lq1_ref[...] * lk1_ref[...], axis=-1, keepdims=True))
           - jnp.exp(jnp.sum(lq2_ref[...] * lk2_ref[...], axis=-1, keepdims=True))
           + lambda_init)
    p = acc_ref[...] / jnp.sum(den_ref[...], axis=-1, keepdims=True)
    half_rows = GQA_GROUP * bq
    o = p[:half_rows] - lam * p[half_rows:]
    for g in range(GQA_GROUP):
        og = _rms(o[g * bq:(g + 1) * bq], sg_ref[...]) * (1.0 - lambda_init)
        o_ref[:, g * HEAD_DIM:(g + 1) * HEAD_DIM] = og.astype(o_ref.dtype)


def _diff_attn(qkv, lq1, lk1, lq2, lk2, subln_g, lambda_init):
    bsz, seq, _ = qkv.shape
    assert DIFF_KC == 4 * HEAD_DIM and seq % DIFF_KC == 0
    bq = DIFF_BQ
    rows = 2 * GQA_GROUP * bq
    gw = GQA_GROUP * HEAD_DIM
    qaug, dbias, kaug = _diff_tables(seq)
    vec = lambda w: pl.BlockSpec((1, w), lambda b, h, n: (0, 0))
    per_head = lambda w: pl.BlockSpec((None, rows, w), lambda b, h, n: (h, 0, 0))
    q_spec = pl.BlockSpec((None, bq, gw), lambda b, h, n: (b, n, h))
    k_spec = pl.BlockSpec((None, seq, HEAD_DIM), lambda b, h, n: (b, 0, N_HEADS + h))
    v_spec = pl.BlockSpec((None, seq, HEAD_DIM), lambda b, h, n: (b, 0, N_HEADS + N_KV_HEADS + h))
    return pl.pallas_call(
        functools.partial(_diff_attn_kernel, lambda_init=lambda_init),
        out_shape=jax.ShapeDtypeStruct((bsz, seq, N_HEADS * HEAD_DIM), BF16),
        grid=(bsz, N_KV_HEADS, seq // bq),
        in_specs=[vec(DIFF_HEAD_DIM), vec(DIFF_HEAD_DIM), vec(DIFF_HEAD_DIM), vec(DIFF_HEAD_DIM),
                  vec(HEAD_DIM), per_head(HEAD_DIM), per_head(bq),
                  pl.BlockSpec((2 * seq, HEAD_DIM), lambda b, h, n: (0, 0)),
                  q_spec, k_spec, v_spec],
        out_specs=q_spec,
        scratch_shapes=[
            pltpu.VMEM((seq, 2 * HEAD_DIM), BF16),
            pltpu.VMEM((rows, 2 * HEAD_DIM), BF16),
            pltpu.VMEM((seq // DIFF_KC, rows, DIFF_KC), F32),
            pltpu.VMEM((rows, HEAD_DIM), F32),
            pltpu.VMEM((rows, HEAD_DIM), F32),
            pltpu.VMEM((rows, HEAD_DIM), F32),
        ],
        compiler_params=_params("parallel", "arbitrary", "arbitrary"),
        name="diff_attn",
    )(lq1, lk1, lq2, lk2, subln_g, qaug, dbias, kaug, qkv, qkv, qkv)


def kernel(x, mix_pre_g, mix_post_g, ffn_pre_g, ffn_post_g, w_gate, w_up, w_down, a_w_qkv, a_w_o, a_sink, b_w_qkv, b_w_o, c_w_qkv, c_w_o, c_lambda_q1, c_lambda_k1, c_lambda_q2, c_lambda_k2, c_subln_g):
    bsz, seq, dm = x.shape
    n = bsz * seq
    slopes = jnp.asarray(_alibi_slopes_np(), F32)
    xf = x.reshape(n, dm)
    row = lambda a, i: a[i].reshape(1, -1)
    qkv_stacks, o_stacks = (a_w_qkv, b_w_qkv, c_w_qkv), (a_w_o, b_w_o, c_w_o)
    w_qkv, w_o = a_w_qkv[:1].astype(BF16), a_w_o[:1].astype(BF16)
    ffn_w = tuple(w[:1].astype(BF16) for w in (w_gate, w_up, w_down))
    q_cols = N_HEADS * HEAD_DIM
    delta = None

    for i in range(DEPTH):
        m, j = i % N_MIXERS, i // N_MIXERS
        pre = row(mix_pre_g, i)
        head_dim = DIFF_HEAD_DIM if m == 2 else HEAD_DIM
        qkv = _norm_matmul(xf, delta, pre, w_qkv, q_cols, LOG2E * head_dim ** -0.5).reshape(bsz, seq, -1)
        lse = None
        if m == 0:
            o = _window_attn(qkv, slopes, a_sink[j])
        elif m == 1:
            o, lse = _dilated_attn(qkv, slopes)
            lse = lse.reshape(n, dm)
        else:
            lambda_init = 0.8 - 0.6 * math.exp(-0.3 * i)
            o = _diff_attn(qkv, row(c_lambda_q1, j), row(c_lambda_k1, j), row(c_lambda_q2, j),
                           row(c_lambda_k2, j), row(c_subln_g, j), lambda_init)
        xf, h = _out_proj(o.reshape(n, dm), w_o, row(mix_post_g, i), row(ffn_pre_g, i), xf, delta, lse=lse)
        if i + 1 < DEPTH:
            m1, j1 = (i + 1) % N_MIXERS, (i + 1) // N_MIXERS
            delta, ffn_w, (w_qkv, w_o) = _ffn(h, *ffn_w, row(ffn_post_g, i),
                                              next_ffn=(w_gate, w_up, w_down, i + 1),
                                              slab_casts=((qkv_stacks[m1], j1), (o_stacks[m1], j1)))
        else:
            delta, _, _ = _ffn(h, *ffn_w, row(ffn_post_g, i))
    xf = _add(xf, delta)
    return xf.reshape(bsz, seq, dm)
```

```python
import functools
import math

import jax
import jax.numpy as jnp
import ml_dtypes
import numpy as np
from jax import lax
from jax.experimental import pallas as pl
from jax.experimental.pallas import tpu as pltpu

F32 = jnp.float32
BF16 = jnp.bfloat16

D_MODEL = 2048
DEPTH = 4
N_MIXERS = 3
HEAD_DIM = 128
N_HEADS = 16
N_KV_HEADS = 4
GQA_GROUP = N_HEADS // N_KV_HEADS
DIFF_HEAD_DIM = HEAD_DIM // 2
WINDOW_A = 128
DILATED_PATTERNS = ((128, 1), (512, 4), (2048, 16))
B_GROUP_HEADS = (6, 5, 5)
FFN_HIDDEN = 5632
RMS_EPS = 1e-6
NEG_INF = -1e30
LOG2E = math.log2(math.e)

VMEM_LIMIT_BYTES = 48 * 1024 * 1024
WIDE_VMEM_LIMIT_BYTES = 60 * 1024 * 1024
FFN_VMEM_LIMIT_BYTES = WIDE_VMEM_LIMIT_BYTES

PROJ_TM, PROJ_TN, PROJ_TN_RESIDUAL = 1024, 1024, 1024
NORM_ROWS = 256
OUT_TM = 512
OUT_ROW_CHUNKS = 4
FFN_TM, FFN_TH = 1024, 512
SLAB_SPLIT = 4
LOCAL_BQ = 128
LOCAL_GROUP = 8
DIFF_BQ = 128
DIFF_KC = 512
DIFF_UNROLL = 4
N_AUG = 3


def _alibi_slopes_np():
    return 2.0 ** (-8.0 * np.arange(1, N_HEADS + 1, dtype=np.float64) / N_HEADS)


def _rms(x, g):
    return x * lax.rsqrt(jnp.mean(x * x, axis=-1, keepdims=True) + RMS_EPS) * g


def _params(*sem, vmem_limit_bytes=VMEM_LIMIT_BYTES):
    return pltpu.CompilerParams(dimension_semantics=sem, vmem_limit_bytes=vmem_limit_bytes)


def _norm_matmul_kernel(*refs, scaled_tiles, scale, has_delta):
    if has_delta:
        x_ref, d_ref, g_ref, w_ref, o_ref, h_ref = refs
    else:
        x_ref, g_ref, w_ref, o_ref, h_ref = refs
    j = pl.program_id(1)

    @pl.when(j == 0)
    def _():
        def norm_rows(r, carry):
            rows = pl.ds(pl.multiple_of(r * NORM_ROWS, NORM_ROWS), NORM_ROWS)
            x = x_ref[rows, :]
            if has_delta:
                x = x + d_ref[rows, :]
            h_ref[rows, :] = _rms(x, g_ref[...]).astype(BF16)
            return carry

        lax.fori_loop(0, x_ref.shape[0] // NORM_ROWS, norm_rows, 0)

    acc = jnp.dot(h_ref[...], w_ref[...], preferred_element_type=F32)
    if scaled_tiles:
        acc = acc * jnp.where(j < scaled_tiles, scale, 1.0)
    o_ref[...] = acc.astype(o_ref.dtype)


def _norm_matmul(x, delta, g, w, scaled_cols=0, scale=1.0):
    n, dm = x.shape
    m = w.shape[2]
    has_delta = delta is not None
    tn = PROJ_TN_RESIDUAL if has_delta else PROJ_TN
    assert scaled_cols % tn == 0
    x_spec = pl.BlockSpec((PROJ_TM, dm), lambda i, j: (i, 0))
    return pl.pallas_call(
        functools.partial(_norm_matmul_kernel, scaled_tiles=scaled_cols // tn, scale=scale,
                          has_delta=has_delta),
        out_shape=jax.ShapeDtypeStruct((n, m), BF16),
        grid=(n // PROJ_TM, m // tn),
        in_specs=[x_spec] * (1 + has_delta) + [
            pl.BlockSpec((1, dm), lambda i, j: (0, 0)),
            pl.BlockSpec((None, dm, tn), lambda i, j: (0, 0, j)),
        ],
        out_specs=pl.BlockSpec((PROJ_TM, tn), lambda i, j: (i, j)),
        scratch_shapes=[pltpu.VMEM((PROJ_TM, dm), BF16)],
        compiler_params=_params("parallel", "arbitrary", vmem_limit_bytes=WIDE_VMEM_LIMIT_BYTES),
        name="norm_matmul",
    )(*([x] + [delta] * has_delta + [g, w]))


def _out_tail(y, rows, g_ref, gn_ref, x_ref, rest):
    out_ref, hn_ref = rest[-2:]
    x = x_ref[rows, :]
    if len(rest) == 3:
        x = x + rest[0][rows, :]
    x1 = x + _rms(y, g_ref[...])
    out_ref[rows, :] = x1
    hn_ref[rows, :] = _rms(x1, gn_ref[...]).astype(BF16)


def _out_kernel(o_ref, w_ref, g_ref, gn_ref, x_ref, *rest):
    chunk = o_ref.shape[0] // OUT_ROW_CHUNKS
    for r0 in range(0, o_ref.shape[0], chunk):
        rows = slice(r0, r0 + chunk)
        y = jnp.dot(o_ref[rows, :], w_ref[...], preferred_element_type=F32)
        _out_tail(y, rows, g_ref, gn_ref, x_ref, rest)


def _out_merge_kernel(o_ref, lse_ref, w_ref, g_ref, gn_ref, x_ref, *rest):
    scores = []
    h0 = 0
    for hg in B_GROUP_HEADS:
        acc = lse_ref[:, h0 * HEAD_DIM:(h0 + 1) * HEAD_DIM]
        for h in range(h0 + 1, h0 + hg):
            acc = acc + lse_ref[:, h * HEAD_DIM:(h + 1) * HEAD_DIM]
        scores.append(acc * (1.0 / hg))
        h0 += hg
    mx = jnp.maximum(jnp.maximum(scores[0], scores[1]), scores[2])
    es = [jnp.exp(s - mx) for s in scores]
    inv = 1.0 / (es[0] + es[1] + es[2])
    pieces = []
    h0 = 0
    for gi, hg in enumerate(B_GROUP_HEADS):
        alpha = es[gi] * inv
        for h in range(h0, h0 + hg):
            oh = o_ref[:, h * HEAD_DIM:(h + 1) * HEAD_DIM].astype(F32)
            pieces.append((oh * alpha).astype(BF16))
        h0 += hg
    mixed = jnp.concatenate(pieces, axis=-1)
    y = jnp.dot(mixed, w_ref[...], preferred_element_type=F32)
    _out_tail(y, slice(None), g_ref, gn_ref, x_ref, rest)


def _out_proj(o, w, g, g_next, x, delta=None, lse=None):
    n, dm = x.shape
    row = lambda i: (i, 0)
    fixed = lambda i: (0, 0)
    o_spec = pl.BlockSpec((OUT_TM, dm), row)
    w_spec = pl.BlockSpec((None, dm, dm), lambda i: (0, 0, 0))
    g_spec = pl.BlockSpec((1, dm), fixed)
    if lse is None:
        body, args, in_specs = _out_kernel, [o], [o_spec]
    else:
        body, args, in_specs = _out_merge_kernel, [o, lse], [o_spec, o_spec]
    args += [w, g, g_next, x] + ([] if delta is None else [delta])
    in_specs += [w_spec, g_spec, g_spec, o_spec] + ([] if delta is None else [o_spec])
    return pl.pallas_call(
        body,
        out_shape=[jax.ShapeDtypeStruct((n, dm), F32), jax.ShapeDtypeStruct((n, dm), BF16)],
        grid=(n // OUT_TM,),
        in_specs=in_specs,
        out_specs=[o_spec, o_spec],
        compiler_params=_params("parallel", vmem_limit_bytes=WIDE_VMEM_LIMIT_BYTES),
        name="out_proj",
    )(*args)


def _ffn_kernel(h_ref, wg_ref, wu_ref, wd_ref, gpost_ref, *rest, n_tile_casts, n_slab_casts):
    n_casts = n_tile_casts + n_slab_casts
    cast_in, out_ref, cast_out = rest[:n_casts], rest[n_casts], rest[n_casts + 1:]
    k = pl.program_id(1)

    for src, dst in zip(cast_in[:n_tile_casts], cast_out[:n_tile_casts]):
        dst[...] = src[...].astype(BF16)

    @pl.when(k == 0)
    def _():
        out_ref[...] = jnp.zeros_like(out_ref)

    @pl.when(k < SLAB_SPLIT)
    def _():
        for src, dst in zip(cast_in[n_tile_casts:], cast_out[n_tile_casts:]):
            dst[...] = src[...].astype(BF16)

    h = h_ref[...]
    gate = jnp.dot(h, wg_ref[...], preferred_element_type=F32)
    up = jnp.dot(h, wu_ref[...], preferred_element_type=F32)
    act = (gate * jax.nn.sigmoid(gate) * up).astype(BF16)
    out_ref[...] += jnp.dot(act, wd_ref[...], preferred_element_type=F32)

    @pl.when(k == pl.num_programs(1) - 1)
    def _():
        out_ref[...] = _rms(out_ref[...], gpost_ref[...])


def _ffn(h, wg, wu, wd, gpost, next_ffn=None, slab_casts=()):
    n, dm = h.shape
    hid = wg.shape[2]
    grid = (n // FFN_TM, hid // FFN_TH)
    assert grid[1] >= SLAB_SPLIT
    row = lambda i, k: (i, 0)
    fixed = lambda i, k: (0, 0)
    in_specs = [
        pl.BlockSpec((FFN_TM, dm), row),
        pl.BlockSpec((None, dm, FFN_TH), lambda i, k: (0, 0, k)),
        pl.BlockSpec((None, dm, FFN_TH), lambda i, k: (0, 0, k)),
        pl.BlockSpec((None, FFN_TH, dm), lambda i, k: (0, k, 0)),
        pl.BlockSpec((1, dm), fixed),
    ]
    args = [h, wg, wu, wd, gpost]
    out_shape = [jax.ShapeDtypeStruct((n, dm), F32)]
    out_specs = [pl.BlockSpec((FFN_TM, dm), row)]

    def add_cast(stack, block, in_map, out_map):
        in_specs.append(pl.BlockSpec((None,) + block, in_map))
        args.append(stack)
        out_shape.append(jax.ShapeDtypeStruct((1,) + stack.shape[1:], BF16))
        out_specs.append(pl.BlockSpec((None,) + block, out_map))

    n_tile_casts = 0
    if next_ffn is not None:
        ng, nu, nd, layer = next_ffn
        cr = dm // grid[0]
        assert dm % grid[0] == 0 and cr % 8 == 0
        add_cast(ng, (cr, FFN_TH), lambda i, k: (layer, i, k), lambda i, k: (0, i, k))
        add_cast(nu, (cr, FFN_TH), lambda i, k: (layer, i, k), lambda i, k: (0, i, k))
        add_cast(nd, (FFN_TH, cr), lambda i, k: (layer, k, i), lambda i, k: (0, k, i))
        n_tile_casts = 3
    for stack, index in slab_casts:
        rows = stack.shape[1] // (grid[0] * SLAB_SPLIT)
        assert stack.shape[1] % (grid[0] * SLAB_SPLIT) == 0 and rows % 8 == 0
        slab = lambda i, k: SLAB_SPLIT * i + jnp.minimum(k, SLAB_SPLIT - 1)
        add_cast(stack, (rows, stack.shape[2]),
                 lambda i, k, index=index: (index, slab(i, k), 0), lambda i, k: (0, slab(i, k), 0))
    res = pl.pallas_call(
        functools.partial(_ffn_kernel, n_tile_casts=n_tile_casts, n_slab_casts=len(slab_casts)),
        out_shape=out_shape,
        grid=grid,
        in_specs=in_specs,
        out_specs=out_specs,
        compiler_params=_params("parallel", "arbitrary", vmem_limit_bytes=FFN_VMEM_LIMIT_BYTES),
        name="ffn",
    )(*args)
    return res[0], tuple(res[1:1 + n_tile_casts]), tuple(res[1 + n_tile_casts:])


def _add_kernel(a_ref, b_ref, o_ref):
    o_ref[...] = a_ref[...] + b_ref[...]


def _add(a, b):
    n, dm = a.shape
    spec = pl.BlockSpec((FFN_TM, dm), lambda i: (i, 0))
    return pl.pallas_call(
        _add_kernel,
        out_shape=jax.ShapeDtypeStruct((n, dm), F32),
        grid=(n // FFN_TM,),
        in_specs=[spec, spec],
        out_specs=spec,
        compiler_params=_params("parallel"),
        name="residual_add",
    )(a, b)


def _local_attn_body(slope, sink, q_ref, k_ref, v_ref, o_ref, lse_ref, bm_ref, scratch, *, half, dil):
    seq = q_ref.shape[0]
    bq = LOCAL_BQ
    lc = seq // dil
    ks = min(3 * bq, lc)
    nqb = lc // bq
    if dil > 1:
        qf, kf, vf, of = scratch
        qf[...] = q_ref[...].astype(F32)
        kf[...] = k_ref[...].astype(F32)
        vf[...] = v_ref[...].astype(F32)
    rel0 = (lax.broadcasted_iota(jnp.int32, (bq, ks), 0)
            - lax.broadcasted_iota(jnp.int32, (bq, ks), 1))
    step_slope = slope * (dil * LOG2E)
    for case in range(3):
        dist = jnp.abs(rel0 + case * bq)
        bm_ref[case, :, :ks] = jnp.where(dist <= half, -step_slope * dist.astype(F32), NEG_INF)
    sink2 = None if sink is None else sink * LOG2E

    def logits(idx):
        r = idx // nqb
        q0 = (idx % nqb) * bq
        k0 = jnp.clip(q0 - bq, 0, lc - ks)
        if dil == 1:
            rows = pl.ds(pl.multiple_of(q0, bq), bq)
            q = q_ref[rows, :]
            k = k_ref[pl.ds(pl.multiple_of(k0, bq), ks), :]
            v = v_ref[pl.ds(pl.multiple_of(k0, bq), ks), :]
        else:
            rows = pl.ds(r + dil * q0, bq, stride=dil)
            q = qf[rows, :].astype(BF16)
            k = kf[pl.ds(r + dil * k0, ks, stride=dil), :].astype(BF16)
            v = vf[pl.ds(r + dil * k0, ks, stride=dil), :].astype(BF16)
        s = lax.dot_general(q, k, (((1,), (1,)), ((), ())), preferred_element_type=F32)
        return s + bm_ref[(q0 - k0) // bq, :, :ks], v, rows

    def softmax(s):
        mx = jnp.max(s, axis=-1, keepdims=True)
        if sink2 is not None:
            mx = jnp.maximum(mx, sink2)
        e = jnp.exp2(s - mx)
        den = jnp.sum(e, axis=-1, keepdims=True)
        if sink2 is not None:
            den = den + jnp.exp2(sink2 - mx)
        return e.astype(BF16), mx, den

    def finish(e, mx, den, v, rows):
        o = jnp.dot(e, v, preferred_element_type=F32) / den
        if dil == 1:
            o_ref[rows, :] = o.astype(o_ref.dtype)
        else:
            of[rows, :] = o
        if lse_ref is not None:
            lse = mx * (1.0 / LOG2E) + jnp.log(den)
            lse_ref[rows, :] = jnp.broadcast_to(lse, (bq, HEAD_DIM))

    def group(gidx, carry):
        staged = [logits(gidx * LOCAL_GROUP + u) for u in range(LOCAL_GROUP)]
        probs = [softmax(s) for s, _, _ in staged]
        for (e, mx, den), (_, v, rows) in zip(probs, staged):
            finish(e, mx, den, v, rows)
        return carry

    lax.fori_loop(0, dil * nqb // LOCAL_GROUP, group, 0)
    if dil > 1:
        o_ref[...] = of[...].astype(o_ref.dtype)


def _window_attn_kernel(slope_ref, sink_ref, q_ref, k_ref, v_ref, *rest, n_casts):
    cast_in, o_ref, cast_out, bm_ref = rest[:n_casts], rest[n_casts], rest[n_casts + 1:-1], rest[-1]
    for src, dst in zip(cast_in, cast_out):
        dst[...] = src[...].astype(BF16)
    head = pl.program_id(1)
    _local_attn_body(slope_ref[head], sink_ref[head], q_ref, k_ref, v_ref, o_ref, None, bm_ref, None,
                     half=WINDOW_A, dil=1)


def _dilated_attn_kernel(slope_ref, q_ref, k_ref, v_ref, o_ref, lse_ref, bm_ref, *scratch):
    head = pl.program_id(1)
    h0 = 0
    for (window, dil), hg in zip(DILATED_PATTERNS, B_GROUP_HEADS):
        @pl.when((head >= h0) & (head < h0 + hg))
        def _(window=window, dil=dil):
            _local_attn_body(slope_ref[head], None, q_ref, k_ref, v_ref, o_ref, lse_ref, bm_ref, scratch,
                             half=window // (2 * dil), dil=dil)
        h0 += hg


_BIAS_TILES = pltpu.VMEM((3, LOCAL_BQ, 3 * LOCAL_BQ), F32)


def _head_spec(seq, col0, group=1):
    return pl.BlockSpec((None, seq, HEAD_DIM), lambda b, h: (b, 0, col0 + h // group))


def _window_attn(qkv, slopes, sink, slab_casts=()):
    bsz, seq, _ = qkv.shape
    smem = pl.BlockSpec(memory_space=pltpu.SMEM)
    steps = bsz * N_HEADS
    in_specs = [smem, smem, _head_spec(seq, 0), _head_spec(seq, N_HEADS, GQA_GROUP),
                _head_spec(seq, N_HEADS + N_KV_HEADS, GQA_GROUP)]
    args = [slopes, sink, qkv, qkv, qkv]
    out_shape = [jax.ShapeDtypeStruct((bsz, seq, N_HEADS * HEAD_DIM), BF16)]
    out_specs = [_head_spec(seq, 0)]
    for stack, index in slab_casts:
        rows = stack.shape[1] // steps
        assert stack.shape[1] % steps == 0 and rows % 16 == 0
        in_specs.append(pl.BlockSpec((None, rows, stack.shape[2]),
                                     lambda b, h, index=index: (index, b * N_HEADS + h, 0)))
        args.append(stack)
        out_shape.append(jax.ShapeDtypeStruct((1,) + stack.shape[1:], BF16))
        out_specs.append(pl.BlockSpec((None, rows, stack.shape[2]), lambda b, h: (0, b * N_HEADS + h, 0)))
    res = pl.pallas_call(
        functools.partial(_window_attn_kernel, n_casts=len(slab_casts)),
        out_shape=out_shape,
        grid=(bsz, N_HEADS),
        in_specs=in_specs,
        out_specs=out_specs,
        scratch_shapes=[_BIAS_TILES],
        compiler_params=_params("parallel", "arbitrary"),
        name="window_attn",
    )(*args)
    return res[0], tuple(res[1:])


def _dilated_attn(qkv, slopes):
    bsz, seq, _ = qkv.shape
    width = N_HEADS * HEAD_DIM
    return pl.pallas_call(
        _dilated_attn_kernel,
        out_shape=[jax.ShapeDtypeStruct((bsz, seq, width), BF16),
                   jax.ShapeDtypeStruct((bsz, seq, width), F32)],
        grid=(bsz, N_HEADS),
        in_specs=[pl.BlockSpec(memory_space=pltpu.SMEM), _head_spec(seq, 0),
                  _head_spec(seq, N_HEADS), _head_spec(seq, 2 * N_HEADS)],
        out_specs=[_head_spec(seq, 0), _head_spec(seq, 0)],
        scratch_shapes=[_BIAS_TILES] + [pltpu.VMEM((seq, HEAD_DIM), F32) for _ in range(4)],
        compiler_params=_params("parallel", "arbitrary"),
        name="dilated_attn",
    )(slopes, qkv, qkv, qkv)


AUG_SIGN, AUG_HI, AUG_LO = 0, N_AUG, 2 * N_AUG


def _split_bf16(x, n=N_AUG):
    terms, rest = [], np.asarray(x, np.float64)
    for _ in range(n):
        t = rest.astype(ml_dtypes.bfloat16).astype(np.float64)
        terms.append(t)
        rest = rest - t
    return terms


def _diff_tables(seq):
    bq = DIFF_BQ
    m2 = _alibi_slopes_np() * LOG2E
    t = np.arange(bq, dtype=np.float64)
    qaug = np.zeros((N_KV_HEADS, 2, GQA_GROUP, bq, HEAD_DIM), np.float64)
    dbias = np.zeros((N_KV_HEADS, 2, GQA_GROUP, bq, bq), np.float64)
    for kvh in range(N_KV_HEADS):
        for g in range(GQA_GROUP):
            m = m2[kvh * GQA_GROUP + g]
            for i, term in enumerate(_split_bf16(m * t)):
                qaug[kvh, :, g, :, AUG_SIGN + i] = term
            for i, term in enumerate(_split_bf16(-float(bq) * m)):
                qaug[kvh, :, g, :, AUG_HI + i] = term
            for i, term in enumerate(_split_bf16(-m)):
                qaug[kvh, :, g, :, AUG_LO + i] = term
            dbias[kvh, :, g] = -m * np.abs(t[:, None] - t[None, :])
    rows = 2 * GQA_GROUP * bq
    qaug = qaug.reshape(N_KV_HEADS, rows, HEAD_DIM).astype(ml_dtypes.bfloat16)
    dbias = dbias.reshape(N_KV_HEADS, rows, bq).astype(np.float32)
    rel = np.arange(-seq, seq, dtype=np.int64)
    hi = np.floor_divide(rel, bq)
    lo = rel - hi * bq
    sign = np.where(rel >= bq, 1, np.where(rel < 0, -1, 0))
    kaug = np.zeros((2 * seq, HEAD_DIM), np.float64)
    for i in range(N_AUG):
        kaug[:, AUG_SIGN + i] = sign
        kaug[:, AUG_HI + i] = sign * hi
        kaug[:, AUG_LO + i] = sign * lo
    return jnp.asarray(qaug), jnp.asarray(dbias), jnp.asarray(kaug.astype(ml_dtypes.bfloat16))


def _lane_tiles(x):
    return [x[:, j * HEAD_DIM:(j + 1) * HEAD_DIM] for j in range(x.shape[1] // HEAD_DIM)]


def _diff_attn_kernel(lq1_ref, lk1_ref, lq2_ref, lk2_ref, sg_ref, qaug_ref, dbias_ref, kaug_ref,
                      q_ref, k_ref, v_ref, o_ref,
                      kp_ref, qp_ref, l_ref, mx_ref, den_ref, acc_ref, *, lambda_init):
    bq = DIFF_BQ
    seq = k_ref.shape[0]
    nchunk = seq // DIFF_KC
    n = pl.program_id(2)
    t0 = n * bq
    dn = (((1,), (1,)), ((), ()))

    @pl.when(n == 0)
    def _():
        kp_ref[:, :HEAD_DIM] = k_ref[...]

    kp_ref[:, HEAD_DIM:] = kaug_ref[pl.ds(pl.multiple_of(seq - t0, bq), seq), :]

    lane = lax.broadcasted_iota(jnp.int32, (bq, HEAD_DIM), 1)
    for c in range(2):
        keep = (lane < DIFF_HEAD_DIM) if c == 0 else (lane >= DIFF_HEAD_DIM)
        for g in range(GQA_GROUP):
            qg = q_ref[:, g * HEAD_DIM:(g + 1) * HEAD_DIM]
            r0 = (c * GQA_GROUP + g) * bq
            qp_ref[r0:r0 + bq, :HEAD_DIM] = jnp.where(keep, qg, jnp.zeros_like(qg))
    qp_ref[:, HEAD_DIM:] = qaug_ref[...]

    tiles_per_chunk = DIFF_KC // HEAD_DIM
    for c in range(nchunk):
        l = lax.dot_general(qp_ref[...], kp_ref[c * DIFF_KC:(c + 1) * DIFF_KC, :], dn,
                            preferred_element_type=F32)
        tiles = []
        for j, lt in enumerate(_lane_tiles(l)):
            own = (c * tiles_per_chunk + j) == n
            tiles.append(lt + jnp.where(own, dbias_ref[...], 0.0))
        l_ref[c] = jnp.concatenate(tiles, axis=-1)
        m = jnp.maximum(jnp.maximum(tiles[0], tiles[1]), jnp.maximum(tiles[2], tiles[3]))
        mx_ref[...] = m if c == 0 else jnp.maximum(mx_ref[...], m)
    mx_ref[...] = jnp.broadcast_to(jnp.max(mx_ref[...], axis=-1, keepdims=True), mx_ref.shape)
    den_ref[...] = jnp.zeros_like(den_ref)
    acc_ref[...] = jnp.zeros_like(acc_ref)

    def pv_chunk(c, carry):
        mxb = mx_ref[...]
        es = [jnp.exp2(lt - mxb) for lt in _lane_tiles(l_ref[c])]
        den_ref[...] += (es[0] + es[1]) + (es[2] + es[3])
        e = jnp.concatenate([x.astype(BF16) for x in es], axis=-1)
        vc = v_ref[pl.ds(pl.multiple_of(c * DIFF_KC, DIFF_KC), DIFF_KC), :]
        acc_ref[...] += jnp.dot(e, vc, preferred_element_type=F32)
        return carry

    lax.fori_loop(0, nchunk, pv_chunk, 0, unroll=DIFF_UNROLL)

    lam = (jnp.exp(jnp.sum(lq1_ref[...] * lk1_ref[...], axis=-1, keepdims=True))
           - jnp.exp(jnp.sum(lq2_ref[...] * lk2_ref[...], axis=-1, keepdims=True))
           + lambda_init)
    p = acc_ref[...] / jnp.sum(den_ref[...], axis=-1, keepdims=True)
    half_rows = GQA_GROUP * bq
    o = p[:half_rows] - lam * p[half_rows:]
    for g in range(GQA_GROUP):
        og = _rms(o[g * bq:(g + 1) * bq], sg_ref[...]) * (1.0 - lambda_init)
        o_ref[:, g * HEAD_DIM:(g + 1) * HEAD_DIM] = og.astype(o_ref.dtype)


def _diff_attn(qkv, lq1, lk1, lq2, lk2, subln_g, lambda_init):
    bsz, seq, _ = qkv.shape
    assert DIFF_KC == 4 * HEAD_DIM and seq % DIFF_KC == 0
    bq = DIFF_BQ
    rows = 2 * GQA_GROUP * bq
    gw = GQA_GROUP * HEAD_DIM
    qaug, dbias, kaug = _diff_tables(seq)
    vec = lambda w: pl.BlockSpec((1, w), lambda b, h, n: (0, 0))
    per_head = lambda w: pl.BlockSpec((None, rows, w), lambda b, h, n: (h, 0, 0))
    q_spec = pl.BlockSpec((None, bq, gw), lambda b, h, n: (b, n, h))
    k_spec = pl.BlockSpec((None, seq, HEAD_DIM), lambda b, h, n: (b, 0, N_HEADS + h))
    v_spec = pl.BlockSpec((None, seq, HEAD_DIM), lambda b, h, n: (b, 0, N_HEADS + N_KV_HEADS + h))
    return pl.pallas_call(
        functools.partial(_diff_attn_kernel, lambda_init=lambda_init),
        out_shape=jax.ShapeDtypeStruct((bsz, seq, N_HEADS * HEAD_DIM), BF16),
        grid=(bsz, N_KV_HEADS, seq // bq),
        in_specs=[vec(DIFF_HEAD_DIM), vec(DIFF_HEAD_DIM), vec(DIFF_HEAD_DIM), vec(DIFF_HEAD_DIM),
                  vec(HEAD_DIM), per_head(HEAD_DIM), per_head(bq),
                  pl.BlockSpec((2 * seq, HEAD_DIM), lambda b, h, n: (0, 0)),
                  q_spec, k_spec, v_spec],
        out_specs=q_spec,
        scratch_shapes=[
            pltpu.VMEM((seq, 2 * HEAD_DIM), BF16),
            pltpu.VMEM((rows, 2 * HEAD_DIM), BF16),
            pltpu.VMEM((seq // DIFF_KC, rows, DIFF_KC), F32),
            pltpu.VMEM((rows, HEAD_DIM), F32),
            pltpu.VMEM((rows, HEAD_DIM), F32),
            pltpu.VMEM((rows, HEAD_DIM), F32),
        ],
        compiler_params=_params("parallel", "arbitrary", "arbitrary"),
        name="diff_attn",
    )(lq1, lk1, lq2, lk2, subln_g, qaug, dbias, kaug, qkv, qkv, qkv)


def kernel(x, mix_pre_g, mix_post_g, ffn_pre_g, ffn_post_g, w_gate, w_up, w_down, a_w_qkv, a_w_o, a_sink, b_w_qkv, b_w_o, c_w_qkv, c_w_o, c_lambda_q1, c_lambda_k1, c_lambda_q2, c_lambda_k2, c_subln_g):
    bsz, seq, dm = x.shape
    n = bsz * seq
    slopes = jnp.asarray(_alibi_slopes_np(), F32)
    xf = x.reshape(n, dm)
    row = lambda a, i: a[i].reshape(1, -1)
    qkv_stacks, o_stacks = (a_w_qkv, b_w_qkv, c_w_qkv), (a_w_o, b_w_o, c_w_o)
    w_qkv = a_w_qkv[:1].astype(BF16)
    w_o = ffn_w = None
    q_cols = N_HEADS * HEAD_DIM
    delta = None

    for i in range(DEPTH):
        m, j = i % N_MIXERS, i // N_MIXERS
        pre = row(mix_pre_g, i)
        head_dim = DIFF_HEAD_DIM if m == 2 else HEAD_DIM
        qkv = _norm_matmul(xf, delta, pre, w_qkv, q_cols, LOG2E * head_dim ** -0.5).reshape(bsz, seq, -1)
        lse = None
        if m == 0 and i == 0:
            o, (w_o, *ffn_w) = _window_attn(qkv, slopes, a_sink[j],
                                            slab_casts=((a_w_o, 0), (w_gate, 0), (w_up, 0), (w_down, 0)))
        elif m == 0:
            o, _ = _window_attn(qkv, slopes, a_sink[j])
        elif m == 1:
            o, lse = _dilated_attn(qkv, slopes)
            lse = lse.reshape(n, dm)
        else:
            lambda_init = 0.8 - 0.6 * math.exp(-0.3 * i)
            o = _diff_attn(qkv, row(c_lambda_q1, j), row(c_lambda_k1, j), row(c_lambda_q2, j),
                           row(c_lambda_k2, j), row(c_subln_g, j), lambda_init)
        xf, h = _out_proj(o.reshape(n, dm), w_o, row(mix_post_g, i), row(ffn_pre_g, i), xf, delta, lse=lse)
        if i + 1 < DEPTH:
            m1, j1 = (i + 1) % N_MIXERS, (i + 1) // N_MIXERS
            delta, ffn_w, (w_qkv, w_o) = _ffn(h, *ffn_w, row(ffn_post_g, i),
                                              next_ffn=(w_gate, w_up, w_down, i + 1),
                                              slab_casts=((qkv_stacks[m1], j1), (o_stacks[m1], j1)))
        else:
            delta, _, _ = _ffn(h, *ffn_w, row(ffn_post_g, i))
    xf = _add(xf, delta)
    return xf.reshape(bsz, seq, dm)
```

```python
import functools
import math

import jax
import jax.numpy as jnp
import ml_dtypes
import numpy as np
from jax import lax
from jax.experimental import pallas as pl
from jax.experimental.pallas import tpu as pltpu

F32 = jnp.float32
BF16 = jnp.bfloat16

D_MODEL = 2048
DEPTH = 4
N_MIXERS = 3
HEAD_DIM = 128
N_HEADS = 16
N_KV_HEADS = 4
GQA_GROUP = N_HEADS // N_KV_HEADS
DIFF_HEAD_DIM = HEAD_DIM // 2
WINDOW_A = 128
DILATED_PATTERNS = ((128, 1), (512, 4), (2048, 16))
B_GROUP_HEADS = (6, 5, 5)
FFN_HIDDEN = 5632
RMS_EPS = 1e-6
NEG_INF = -1e30
LOG2E = math.log2(math.e)

VMEM_LIMIT_BYTES = 48 * 1024 * 1024
WIDE_VMEM_LIMIT_BYTES = 60 * 1024 * 1024
FFN_VMEM_LIMIT_BYTES = WIDE_VMEM_LIMIT_BYTES

PROJ_TM, PROJ_TN, PROJ_TN_RESIDUAL = 1024, 1024, 1024
NORM_ROWS = 256
OUT_TM = 512
OUT_ROW_CHUNKS = 4
FFN_TM, FFN_TH = 1024, 512
SLAB_SPLIT = 4
LOCAL_BQ = 128
LOCAL_GROUP = 8
DIFF_BQ = 128
DIFF_KC = 512
DIFF_UNROLL = 4
N_AUG = 3


def _alibi_slopes_np():
    return 2.0 ** (-8.0 * np.arange(1, N_HEADS + 1, dtype=np.float64) / N_HEADS)


def _rms(x, g):
    return x * lax.rsqrt(jnp.mean(x * x, axis=-1, keepdims=True) + RMS_EPS) * g


def _params(*sem, vmem_limit_bytes=VMEM_LIMIT_BYTES):
    return pltpu.CompilerParams(dimension_semantics=sem, vmem_limit_bytes=vmem_limit_bytes)


def _norm_matmul_kernel(*refs, scaled_tiles, scale, has_delta):
    if has_delta:
        x_ref, d_ref, g_ref, w_ref, o_ref, h_ref = refs
    else:
        x_ref, g_ref, w_ref, o_ref, h_ref = refs
    j = pl.program_id(1)

    @pl.when(j == 0)
    def _():
        def norm_rows(r, carry):
            rows = pl.ds(pl.multiple_of(r * NORM_ROWS, NORM_ROWS), NORM_ROWS)
            x = x_ref[rows, :]
            if has_delta:
                x = x + d_ref[rows, :]
            h_ref[rows, :] = _rms(x, g_ref[...]).astype(BF16)
            return carry

        lax.fori_loop(0, x_ref.shape[0] // NORM_ROWS, norm_rows, 0)

    acc = jnp.dot(h_ref[...], w_ref[...], preferred_element_type=F32)
    if scaled_tiles:
        acc = acc * jnp.where(j < scaled_tiles, scale, 1.0)
    o_ref[...] = acc.astype(o_ref.dtype)


def _norm_matmul(x, delta, g, w, scaled_cols=0, scale=1.0):
    n, dm = x.shape
    m = w.shape[2]
    has_delta = delta is not None
    tn = PROJ_TN_RESIDUAL if has_delta else PROJ_TN
    assert scaled_cols % tn == 0
    x_spec = pl.BlockSpec((PROJ_TM, dm), lambda i, j: (i, 0))
    return pl.pallas_call(
        functools.partial(_norm_matmul_kernel, scaled_tiles=scaled_cols // tn, scale=scale,
                          has_delta=has_delta),
        out_shape=jax.ShapeDtypeStruct((n, m), BF16),
        grid=(n // PROJ_TM, m // tn),
        in_specs=[x_spec] * (1 + has_delta) + [
            pl.BlockSpec((1, dm), lambda i, j: (0, 0)),
            pl.BlockSpec((None, dm, tn), lambda i, j: (0, 0, j)),
        ],
        out_specs=pl.BlockSpec((PROJ_TM, tn), lambda i, j: (i, j)),
        scratch_shapes=[pltpu.VMEM((PROJ_TM, dm), BF16)],
        compiler_params=_params("parallel", "arbitrary", vmem_limit_bytes=WIDE_VMEM_LIMIT_BYTES),
        name="norm_matmul",
    )(*([x] + [delta] * has_delta + [g, w]))


def _out_tail(y, rows, g_ref, gn_ref, x_ref, rest):
    out_ref, hn_ref = rest[-2:]
    x = x_ref[rows, :]
    if len(rest) == 3:
        x = x + rest[0][rows, :]
    x1 = x + _rms(y, g_ref[...])
    out_ref[rows, :] = x1
    hn_ref[rows, :] = _rms(x1, gn_ref[...]).astype(BF16)


def _out_kernel(o_ref, w_ref, g_ref, gn_ref, x_ref, *rest):
    chunk = o_ref.shape[0] // OUT_ROW_CHUNKS
    for r0 in range(0, o_ref.shape[0], chunk):
        rows = slice(r0, r0 + chunk)
        y = jnp.dot(o_ref[rows, :], w_ref[...], preferred_element_type=F32)
        _out_tail(y, rows, g_ref, gn_ref, x_ref, rest)


def _out_merge_kernel(o_ref, lse_ref, w_ref, g_ref, gn_ref, x_ref, *rest):
    scores = []
    h0 = 0
    for hg in B_GROUP_HEADS:
        acc = lse_ref[:, h0 * HEAD_DIM:(h0 + 1) * HEAD_DIM]
        for h in range(h0 + 1, h0 + hg):
            acc = acc + lse_ref[:, h * HEAD_DIM:(h + 1) * HEAD_DIM]
        scores.append(acc * (1.0 / hg))
        h0 += hg
    mx = jnp.maximum(jnp.maximum(scores[0], scores[1]), scores[2])
    es = [jnp.exp(s - mx) for s in scores]
    inv = 1.0 / (es[0] + es[1] + es[2])
    pieces = []
    h0 = 0
    for gi, hg in enumerate(B_GROUP_HEADS):
        alpha = es[gi] * inv
        for h in range(h0, h0 + hg):
            oh = o_ref[:, h * HEAD_DIM:(h + 1) * HEAD_DIM].astype(F32)
            pieces.append((oh * alpha).astype(BF16))
        h0 += hg
    mixed = jnp.concatenate(pieces, axis=-1)
    y = jnp.dot(mixed, w_ref[...], preferred_element_type=F32)
    _out_tail(y, slice(None), g_ref, gn_ref, x_ref, rest)


def _out_proj(o, w, g, g_next, x, delta=None, lse=None):
    n, dm = x.shape
    row = lambda i: (i, 0)
    fixed = lambda i: (0, 0)
    o_spec = pl.BlockSpec((OUT_TM, dm), row)
    w_spec = pl.BlockSpec((None, dm, dm), lambda i: (0, 0, 0))
    g_spec = pl.BlockSpec((1, dm), fixed)
    if lse is None:
        body, args, in_specs = _out_kernel, [o], [o_spec]
    else:
        body, args, in_specs = _out_merge_kernel, [o, lse], [o_spec, o_spec]
    args += [w, g, g_next, x] + ([] if delta is None else [delta])
    in_specs += [w_spec, g_spec, g_spec, o_spec] + ([] if delta is None else [o_spec])
    return pl.pallas_call(
        body,
        out_shape=[jax.ShapeDtypeStruct((n, dm), F32), jax.ShapeDtypeStruct((n, dm), BF16)],
        grid=(n // OUT_TM,),
        in_specs=in_specs,
        out_specs=[o_spec, o_spec],
        compiler_params=_params("parallel", vmem_limit_bytes=WIDE_VMEM_LIMIT_BYTES),
        name="out_proj",
    )(*args)


def _ffn_kernel(h_ref, wg_ref, wu_ref, wd_ref, gpost_ref, *rest, n_tile_casts, n_slab_casts):
    n_casts = n_tile_casts + n_slab_casts
    cast_in, out_ref, cast_out = rest[:n_casts], rest[n_casts], rest[n_casts + 1:]
    k = pl.program_id(1)

    for src, dst in zip(cast_in[:n_tile_casts], cast_out[:n_tile_casts]):
        dst[...] = src[...].astype(BF16)

    @pl.when(k == 0)
    def _():
        out_ref[...] = jnp.zeros_like(out_ref)

    @pl.when(k < SLAB_SPLIT)
    def _():
        for src, dst in zip(cast_in[n_tile_casts:], cast_out[n_tile_casts:]):
            dst[...] = src[...].astype(BF16)

    h = h_ref[...]
    gate = jnp.dot(h, wg_ref[...], preferred_element_type=F32)
    up = jnp.dot(h, wu_ref[...], preferred_element_type=F32)
    act = (gate * jax.nn.sigmoid(gate) * up).astype(BF16)
    out_ref[...] += jnp.dot(act, wd_ref[...], preferred_element_type=F32)

    @pl.when(k == pl.num_programs(1) - 1)
    def _():
        out_ref[...] = _rms(out_ref[...], gpost_ref[...])


def _ffn(h, wg, wu, wd, gpost, next_ffn=None, slab_casts=()):
    n, dm = h.shape
    hid = wg.shape[2]
    grid = (n // FFN_TM, hid // FFN_TH)
    assert grid[1] >= SLAB_SPLIT
    row = lambda i, k: (i, 0)
    fixed = lambda i, k: (0, 0)
    in_specs = [
        pl.BlockSpec((FFN_TM, dm), row),
        pl.BlockSpec((None, dm, FFN_TH), lambda i, k: (0, 0, k)),
        pl.BlockSpec((None, dm, FFN_TH), lambda i, k: (0, 0, k)),
        pl.BlockSpec((None, FFN_TH, dm), lambda i, k: (0, k, 0)),
        pl.BlockSpec((1, dm), fixed),
    ]
    args = [h, wg, wu, wd, gpost]
    out_shape = [jax.ShapeDtypeStruct((n, dm), F32)]
    out_specs = [pl.BlockSpec((FFN_TM, dm), row)]

    def add_cast(stack, block, in_map, out_map):
        in_specs.append(pl.BlockSpec((None,) + block, in_map))
        args.append(stack)
        out_shape.append(jax.ShapeDtypeStruct((1,) + stack.shape[1:], BF16))
        out_specs.append(pl.BlockSpec((None,) + block, out_map))

    n_tile_casts = 0
    if next_ffn is not None:
        ng, nu, nd, layer = next_ffn
        cr = dm // grid[0]
        assert dm % grid[0] == 0 and cr % 8 == 0
        add_cast(ng, (cr, FFN_TH), lambda i, k: (layer, i, k), lambda i, k: (0, i, k))
        add_cast(nu, (cr, FFN_TH), lambda i, k: (layer, i, k), lambda i, k: (0, i, k))
        add_cast(nd, (FFN_TH, cr), lambda i, k: (layer, k, i), lambda i, k: (0, k, i))
        n_tile_casts = 3
    for stack, index in slab_casts:
        rows = stack.shape[1] // (grid[0] * SLAB_SPLIT)
        assert stack.shape[1] % (grid[0] * SLAB_SPLIT) == 0 and rows % 8 == 0
        slab = lambda i, k: SLAB_SPLIT * i + jnp.minimum(k, SLAB_SPLIT - 1)
        add_cast(stack, (rows, stack.shape[2]),
                 lambda i, k, index=index: (index, slab(i, k), 0), lambda i, k: (0, slab(i, k), 0))
    res = pl.pallas_call(
        functools.partial(_ffn_kernel, n_tile_casts=n_tile_casts, n_slab_casts=len(slab_casts)),
        out_shape=out_shape,
        grid=grid,
        in_specs=in_specs,
        out_specs=out_specs,
        compiler_params=_params("parallel", "arbitrary", vmem_limit_bytes=FFN_VMEM_LIMIT_BYTES),
        name="ffn",
    )(*args)
    return res[0], tuple(res[1:1 + n_tile_casts]), tuple(res[1 + n_tile_casts:])


def _add_kernel(a_ref, b_ref, o_ref):
    o_ref[...] = a_ref[...] + b_ref[...]


def _add(a, b):
    n, dm = a.shape
    spec = pl.BlockSpec((FFN_TM, dm), lambda i: (i, 0))
    return pl.pallas_call(
        _add_kernel,
        out_shape=jax.ShapeDtypeStruct((n, dm), F32),
        grid=(n // FFN_TM,),
        in_specs=[spec, spec],
        out_specs=spec,
        compiler_params=_params("parallel"),
        name="residual_add",
    )(a, b)


def _local_attn_body(slope, sink, q_ref, k_ref, v_ref, o_ref, lse_ref, bm_ref, scratch, *, half, dil):
    seq = q_ref.shape[0]
    bq = LOCAL_BQ
    lc = seq // dil
    ks = min(3 * bq, lc)
    nqb = lc // bq
    two_step = dil % 8 == 0
    quarter = seq // 4
    stride = dil // 4 if two_step else dil
    if dil > 1:
        qf, kf, vf, of, q4, k4, v4, l4 = scratch
        qf[...] = q_ref[...].astype(F32)
        kf[...] = k_ref[...].astype(F32)
        vf[...] = v_ref[...].astype(F32)
        if two_step:
            for a in range(4):
                blk = slice(a * quarter, (a + 1) * quarter)
                q4[blk, :] = qf[pl.ds(a, quarter, stride=4), :]
                k4[blk, :] = kf[pl.ds(a, quarter, stride=4), :]
                v4[blk, :] = vf[pl.ds(a, quarter, stride=4), :]
            qf, kf, vf, unmix = q4, k4, v4, qf

    def class_rows(r, first, count):
        base = (r % 4) * quarter + r // 4 if two_step else r
        return pl.ds(base + stride * first, count, stride=stride)
    rel0 = (lax.broadcasted_iota(jnp.int32, (bq, ks), 0)
            - lax.broadcasted_iota(jnp.int32, (bq, ks), 1))
    step_slope = slope * (dil * LOG2E)
    for case in range(3):
        dist = jnp.abs(rel0 + case * bq)
        bm_ref[case, :, :ks] = jnp.where(dist <= half, -step_slope * dist.astype(F32), NEG_INF)
    sink2 = None if sink is None else sink * LOG2E

    def logits(idx):
        r = idx // nqb
        q0 = (idx % nqb) * bq
        k0 = jnp.clip(q0 - bq, 0, lc - ks)
        if dil == 1:
            rows = pl.ds(pl.multiple_of(q0, bq), bq)
            q = q_ref[rows, :]
            k = k_ref[pl.ds(pl.multiple_of(k0, bq), ks), :]
            v = v_ref[pl.ds(pl.multiple_of(k0, bq), ks), :]
        else:
            rows = class_rows(r, q0, bq)
            q = qf[rows, :].astype(BF16)
            k = kf[class_rows(r, k0, ks), :].astype(BF16)
            v = vf[class_rows(r, k0, ks), :].astype(BF16)
        s = lax.dot_general(q, k, (((1,), (1,)), ((), ())), preferred_element_type=F32)
        return s + bm_ref[(q0 - k0) // bq, :, :ks], v, rows

    def softmax(s):
        mx = jnp.max(s, axis=-1, keepdims=True)
        if sink2 is not None:
            mx = jnp.maximum(mx, sink2)
        e = jnp.exp2(s - mx)
        den = jnp.sum(e, axis=-1, keepdims=True)
        if sink2 is not None:
            den = den + jnp.exp2(sink2 - mx)
        return e.astype(BF16), mx, den

    def finish(e, mx, den, v, rows):
        o = jnp.dot(e, v, preferred_element_type=F32) / den
        if dil == 1:
            o_ref[rows, :] = o.astype(o_ref.dtype)
        else:
            of[rows, :] = o
        if lse_ref is not None:
            lse = mx * (1.0 / LOG2E) + jnp.log(den)
            (l4 if two_step else lse_ref)[rows, :] = jnp.broadcast_to(lse, (bq, HEAD_DIM))

    def group(gidx, carry):
        staged = [logits(gidx * LOCAL_GROUP + u) for u in range(LOCAL_GROUP)]
        probs = [softmax(s) for s, _, _ in staged]
        for (e, mx, den), (_, v, rows) in zip(probs, staged):
            finish(e, mx, den, v, rows)
        return carry

    lax.fori_loop(0, dil * nqb // LOCAL_GROUP, group, 0)
    if two_step:
        for a in range(4):
            blk = slice(a * quarter, (a + 1) * quarter)
            unmix[pl.ds(a, quarter, stride=4), :] = of[blk, :]
            lse_ref[pl.ds(a, quarter, stride=4), :] = l4[blk, :]
        o_ref[...] = unmix[...].astype(o_ref.dtype)
    elif dil > 1:
        o_ref[...] = of[...].astype(o_ref.dtype)


def _window_attn_kernel(slope_ref, sink_ref, q_ref, k_ref, v_ref, *rest, n_casts):
    cast_in, o_ref, cast_out, bm_ref = rest[:n_casts], rest[n_casts], rest[n_casts + 1:-1], rest[-1]
    for src, dst in zip(cast_in, cast_out):
        dst[...] = src[...].astype(BF16)
    head = pl.program_id(1)
    _local_attn_body(slope_ref[head], sink_ref[head], q_ref, k_ref, v_ref, o_ref, None, bm_ref, None,
                     half=WINDOW_A, dil=1)


def _dilated_attn_kernel(slope_ref, q_ref, k_ref, v_ref, o_ref, lse_ref, bm_ref, *scratch):
    head = pl.program_id(1)
    h0 = 0
    for (window, dil), hg in zip(DILATED_PATTERNS, B_GROUP_HEADS):
        @pl.when((head >= h0) & (head < h0 + hg))
        def _(window=window, dil=dil):
            _local_attn_body(slope_ref[head], None, q_ref, k_ref, v_ref, o_ref, lse_ref, bm_ref, scratch,
                             half=window // (2 * dil), dil=dil)
        h0 += hg


_BIAS_TILES = pltpu.VMEM((3, LOCAL_BQ, 3 * LOCAL_BQ), F32)


def _head_spec(seq, col0, group=1):
    return pl.BlockSpec((None, seq, HEAD_DIM), lambda b, h: (b, 0, col0 + h // group))


def _window_attn(qkv, slopes, sink, slab_casts=()):
    bsz, seq, _ = qkv.shape
    smem = pl.BlockSpec(memory_space=pltpu.SMEM)
    steps = bsz * N_HEADS
    in_specs = [smem, smem, _head_spec(seq, 0), _head_spec(seq, N_HEADS, GQA_GROUP),
                _head_spec(seq, N_HEADS + N_KV_HEADS, GQA_GROUP)]
    args = [slopes, sink, qkv, qkv, qkv]
    out_shape = [jax.ShapeDtypeStruct((bsz, seq, N_HEADS * HEAD_DIM), BF16)]
    out_specs = [_head_spec(seq, 0)]
    for stack, index in slab_casts:
        rows = stack.shape[1] // steps
        assert stack.shape[1] % steps == 0 and rows % 16 == 0
        in_specs.append(pl.BlockSpec((None, rows, stack.shape[2]),
                                     lambda b, h, index=index: (index, b * N_HEADS + h, 0)))
        args.append(stack)
        out_shape.append(jax.ShapeDtypeStruct((1,) + stack.shape[1:], BF16))
        out_specs.append(pl.BlockSpec((None, rows, stack.shape[2]), lambda b, h: (0, b * N_HEADS + h, 0)))
    res = pl.pallas_call(
        functools.partial(_window_attn_kernel, n_casts=len(slab_casts)),
        out_shape=out_shape,
        grid=(bsz, N_HEADS),
        in_specs=in_specs,
        out_specs=out_specs,
        scratch_shapes=[_BIAS_TILES],
        compiler_params=_params("parallel", "arbitrary"),
        name="window_attn",
    )(*args)
    return res[0], tuple(res[1:])


def _dilated_attn(qkv, slopes):
    bsz, seq, _ = qkv.shape
    width = N_HEADS * HEAD_DIM
    return pl.pallas_call(
        _dilated_attn_kernel,
        out_shape=[jax.ShapeDtypeStruct((bsz, seq, width), BF16),
                   jax.ShapeDtypeStruct((bsz, seq, width), F32)],
        grid=(bsz, N_HEADS),
        in_specs=[pl.BlockSpec(memory_space=pltpu.SMEM), _head_spec(seq, 0),
                  _head_spec(seq, N_HEADS), _head_spec(seq, 2 * N_HEADS)],
        out_specs=[_head_spec(seq, 0), _head_spec(seq, 0)],
        scratch_shapes=[_BIAS_TILES] + [pltpu.VMEM((seq, HEAD_DIM), F32) for _ in range(8)],
        compiler_params=_params("parallel", "arbitrary"),
        name="dilated_attn",
    )(slopes, qkv, qkv, qkv)


AUG_SIGN, AUG_HI, AUG_LO = 0, N_AUG, 2 * N_AUG


def _split_bf16(x, n=N_AUG):
    terms, rest = [], np.asarray(x, np.float64)
    for _ in range(n):
        t = rest.astype(ml_dtypes.bfloat16).astype(np.float64)
        terms.append(t)
        rest = rest - t
    return terms


def _diff_tables(seq):
    bq = DIFF_BQ
    m2 = _alibi_slopes_np() * LOG2E
    t = np.arange(bq, dtype=np.float64)
    qaug = np.zeros((N_KV_HEADS, 2, GQA_GROUP, bq, HEAD_DIM), np.float64)
    dbias = np.zeros((N_KV_HEADS, 2, GQA_GROUP, bq, bq), np.float64)
    for kvh in range(N_KV_HEADS):
        for g in range(GQA_GROUP):
            m = m2[kvh * GQA_GROUP + g]
            for i, term in enumerate(_split_bf16(m * t)):
                qaug[kvh, :, g, :, AUG_SIGN + i] = term
            for i, term in enumerate(_split_bf16(-float(bq) * m)):
                qaug[kvh, :, g, :, AUG_HI + i] = term
            for i, term in enumerate(_split_bf16(-m)):
                qaug[kvh, :, g, :, AUG_LO + i] = term
            dbias[kvh, :, g] = -m * np.abs(t[:, None] - t[None, :])
    rows = 2 * GQA_GROUP * bq
    qaug = qaug.reshape(N_KV_HEADS, rows, HEAD_DIM).astype(ml_dtypes.bfloat16)
    dbias = dbias.reshape(N_KV_HEADS, rows, bq).astype(np.float32)
    rel = np.arange(-seq, seq, dtype=np.int64)
    hi = np.floor_divide(rel, bq)
    lo = rel - hi * bq
    sign = np.where(rel >= bq, 1, np.where(rel < 0, -1, 0))
    kaug = np.zeros((2 * seq, HEAD_DIM), np.float64)
    for i in range(N_AUG):
        kaug[:, AUG_SIGN + i] = sign
        kaug[:, AUG_HI + i] = sign * hi
        kaug[:, AUG_LO + i] = sign * lo
    return jnp.asarray(qaug), jnp.asarray(dbias), jnp.asarray(kaug.astype(ml_dtypes.bfloat16))


def _lane_tiles(x):
    return [x[:, j * HEAD_DIM:(j + 1) * HEAD_DIM] for j in range(x.shape[1] // HEAD_DIM)]


def _diff_attn_kernel(lq1_ref, lk1_ref, lq2_ref, lk2_ref, sg_ref, qaug_ref, dbias_ref, kaug_ref,
                      q_ref, k_ref, v_ref, o_ref,
                      kp_ref, qp_ref, l_ref, mx_ref, den_ref, acc_ref, *, lambda_init):
    bq = DIFF_BQ
    seq = k_ref.shape[0]
    nchunk = seq // DIFF_KC
    n = pl.program_id(2)
    t0 = n * bq
    dn = (((1,), (1,)), ((), ()))

    @pl.when(n == 0)
    def _():
        kp_ref[:, :HEAD_DIM] = k_ref[...]

    kp_ref[:, HEAD_DIM:] = kaug_ref[pl.ds(pl.multiple_of(seq - t0, bq), seq), :]

    lane = lax.broadcasted_iota(jnp.int32, (bq, HEAD_DIM), 1)
    for c in range(2):
        keep = (lane < DIFF_HEAD_DIM) if c == 0 else (lane >= DIFF_HEAD_DIM)
        for g in range(GQA_GROUP):
            qg = q_ref[:, g * HEAD_DIM:(g + 1) * HEAD_DIM]
            r0 = (c * GQA_GROUP + g) * bq
            qp_ref[r0:r0 + bq, :HEAD_DIM] = jnp.where(keep, qg, jnp.zeros_like(qg))
    qp_ref[:, HEAD_DIM:] = qaug_ref[...]

    tiles_per_chunk = DIFF_KC // HEAD_DIM
    for c in range(nchunk):
        l = lax.dot_general(qp_ref[...], kp_ref[c * DIFF_KC:(c + 1) * DIFF_KC, :], dn,
                            preferred_element_type=F32)
        tiles = []
        for j, lt in enumerate(_lane_tiles(l)):
            own = (c * tiles_per_chunk + j) == n
            tiles.append(lt + jnp.where(own, dbias_ref[...], 0.0))
        l_ref[c] = jnp.concatenate(tiles, axis=-1)
        m = jnp.maximum(jnp.maximum(tiles[0], tiles[1]), jnp.maximum(tiles[2], tiles[3]))
        mx_ref[...] = m if c == 0 else jnp.maximum(mx_ref[...], m)
    mx_ref[...] = jnp.broadcast_to(jnp.max(mx_ref[...], axis=-1, keepdims=True), mx_ref.shape)
    den_ref[...] = jnp.zeros_like(den_ref)
    acc_ref[...] = jnp.zeros_like(acc_ref)

    def pv_chunk(c, carry):
        mxb = mx_ref[...]
        es = [jnp.exp2(lt - mxb) for lt in _lane_tiles(l_ref[c])]
        den_ref[...] += (es[0] + es[1]) + (es[2] + es[3])
        e = jnp.concatenate([x.astype(BF16) for x in es], axis=-1)
        vc = v_ref[pl.ds(pl.multiple_of(c * DIFF_KC, DIFF_KC), DIFF_KC), :]
        acc_ref[...] += jnp.dot(e, vc, preferred_element_type=F32)
        return carry

    lax.fori_loop(0, nchunk, pv_chunk, 0, unroll=DIFF_UNROLL)

    lam = (jnp.exp(jnp.sum(lq1_ref[...] * lk1_ref[...], axis=-1, keepdims=True))
           - jnp.exp(jnp.sum(lq2_ref[...] * lk2_ref[...], axis=-1, keepdims=True))
           + lambda_init)
    p = acc_ref[...] / jnp.sum(den_ref[...], axis=-1, keepdims=True)
    half_rows = GQA_GROUP * bq
    o = p[:half_rows] - lam * p[half_rows:]
    for g in range(GQA_GROUP):
        og = _rms(o[g * bq:(g + 1) * bq], sg_ref[...]) * (1.0 - lambda_init)
        o_ref[:, g * HEAD_DIM:(g + 1) * HEAD_DIM] = og.astype(o_ref.dtype)


def _diff_attn(qkv, lq1, lk1, lq2, lk2, subln_g, lambda_init):
    bsz, seq, _ = qkv.shape
    assert DIFF_KC == 4 * HEAD_DIM and seq % DIFF_KC == 0
    bq = DIFF_BQ
    rows = 2 * GQA_GROUP * bq
    gw = GQA_GROUP * HEAD_DIM
    qaug, dbias, kaug = _diff_tables(seq)
    vec = lambda w: pl.BlockSpec((1, w), lambda b, h, n: (0, 0))
    per_head = lambda w: pl.BlockSpec((None, rows, w), lambda b, h, n: (h, 0, 0))
    q_spec = pl.BlockSpec((None, bq, gw), lambda b, h, n: (b, n, h))
    k_spec = pl.BlockSpec((None, seq, HEAD_DIM), lambda b, h, n: (b, 0, N_HEADS + h))
    v_spec = pl.BlockSpec((None, seq, HEAD_DIM), lambda b, h, n: (b, 0, N_HEADS + N_KV_HEADS + h))
    return pl.pallas_call(
        functools.partial(_diff_attn_kernel, lambda_init=lambda_init),
        out_shape=jax.ShapeDtypeStruct((bsz, seq, N_HEADS * HEAD_DIM), BF16),
        grid=(bsz, N_KV_HEADS, seq // bq),
        in_specs=[vec(DIFF_HEAD_DIM), vec(DIFF_HEAD_DIM), vec(DIFF_HEAD_DIM), vec(DIFF_HEAD_DIM),
                  vec(HEAD_DIM), per_head(HEAD_DIM), per_head(bq),
                  pl.BlockSpec((2 * seq, HEAD_DIM), lambda b, h, n: (0, 0)),
                  q_spec, k_spec, v_spec],
        out_specs=q_spec,
        scratch_shapes=[
            pltpu.VMEM((seq, 2 * HEAD_DIM), BF16),
            pltpu.VMEM((rows, 2 * HEAD_DIM), BF16),
            pltpu.VMEM((seq // DIFF_KC, rows, DIFF_KC), F32),
            pltpu.VMEM((rows, HEAD_DIM), F32),
            pltpu.VMEM((rows, HEAD_DIM), F32),
            pltpu.VMEM((rows, HEAD_DIM), F32),
        ],
        compiler_params=_params("parallel", "arbitrary", "arbitrary"),
        name="diff_attn",
    )(lq1, lk1, lq2, lk2, subln_g, qaug, dbias, kaug, qkv, qkv, qkv)


def kernel(x, mix_pre_g, mix_post_g, ffn_pre_g, ffn_post_g, w_gate, w_up, w_down, a_w_qkv, a_w_o, a_sink, b_w_qkv, b_w_o, c_w_qkv, c_w_o, c_lambda_q1, c_lambda_k1, c_lambda_q2, c_lambda_k2, c_subln_g):
    bsz, seq, dm = x.shape
    n = bsz * seq
    slopes = jnp.asarray(_alibi_slopes_np(), F32)
    xf = x.reshape(n, dm)
    row = lambda a, i: a[i].reshape(1, -1)
    qkv_stacks, o_stacks = (a_w_qkv, b_w_qkv, c_w_qkv), (a_w_o, b_w_o, c_w_o)
    w_qkv = a_w_qkv[:1].astype(BF16)
    w_o = ffn_w = None
    q_cols = N_HEADS * HEAD_DIM
    delta = None

    for i in range(DEPTH):
        m, j = i % N_MIXERS, i // N_MIXERS
        pre = row(mix_pre_g, i)
        head_dim = DIFF_HEAD_DIM if m == 2 else HEAD_DIM
        qkv = _norm_matmul(xf, delta, pre, w_qkv, q_cols, LOG2E * head_dim ** -0.5).reshape(bsz, seq, -1)
        lse = None
        if m == 0 and i == 0:
            o, (w_o, *ffn_w) = _window_attn(qkv, slopes, a_sink[j],
                                            slab_casts=((a_w_o, 0), (w_gate, 0), (w_up, 0), (w_down, 0)))
        elif m == 0:
            o, _ = _window_attn(qkv, slopes, a_sink[j])
        elif m == 1:
            o, lse = _dilated_attn(qkv, slopes)
            lse = lse.reshape(n, dm)
        else:
            lambda_init = 0.8 - 0.6 * math.exp(-0.3 * i)
            o = _diff_attn(qkv, row(c_lambda_q1, j), row(c_lambda_k1, j), row(c_lambda_q2, j),
                           row(c_lambda_k2, j), row(c_subln_g, j), lambda_init)
        xf, h = _out_proj(o.reshape(n, dm), w_o, row(mix_post_g, i), row(ffn_pre_g, i), xf, delta, lse=lse)
        if i + 1 < DEPTH:
            m1, j1 = (i + 1) % N_MIXERS, (i + 1) // N_MIXERS
            delta, ffn_w, (w_qkv, w_o) = _ffn(h, *ffn_w, row(ffn_post_g, i),
                                              next_ffn=(w_gate, w_up, w_down, i + 1),
                                              slab_casts=((qkv_stacks[m1], j1), (o_stacks[m1], j1)))
        else:
            delta, _, _ = _ffn(h, *ffn_w, row(ffn_post_g, i))
    xf = _add(xf, delta)
    return xf.reshape(bsz, seq, dm)
```
